```python
import math
import jax
import jax.numpy as jnp
from jax import lax
import numpy as np

D_MODEL = 2048
BATCH = 4
SEQ = 2048
DEPTH = 4
DEC_BATCH = 32
DEC_SEQ = 8
PAST_LEN = 16384
PAGE_SIZE = 128

HEAD_DIM = 64
A_HEADS = 4
A_DK = 128
A_DV = 128
A_QK = A_HEADS * A_DK
A_VW = A_HEADS * A_DV
A_CONV_CH = 2 * A_QK + A_VW
CONV_W = 4
GDN_CHUNK = 64
B_HEADS = 8
B_WIDTH = B_HEADS * HEAD_DIM
B_BRANCHES = ((128, 1), (512, 4), (2048, 16))
B_CACHE = 2048
C_WIDTH = 512
C_GROUP = 16
C_GROUPS = C_WIDTH // C_GROUP
C_STATE = 64
D_HEADS = 8
D_KV_HEADS = 2
D_GROUP = D_HEADS // D_KV_HEADS
D_WIDTH = D_HEADS * HEAD_DIM
D_KV_WIDTH = D_KV_HEADS * HEAD_DIM
D_WINDOW = 128
BAND = 128
MIX_WIDTH = A_VW + B_WIDTH + C_WIDTH + D_WIDTH
IN_SIZES = (A_CONV_CH, A_VW, A_HEADS, A_HEADS, 3 * B_WIDTH, C_WIDTH, D_WIDTH, D_KV_WIDTH, D_KV_WIDTH)
IN_COLS = sum(IN_SIZES)
PEER_HEADS = 8
PEER_HALF = 128
PEER_QDIM = 2 * PEER_HALF
N_KEYS = 128
N_EXPERTS = N_KEYS * N_KEYS
PEER_TOPK = 16
PEER_BLOCK = 128
DEEPNORM_ALPHA = (2 * DEPTH) ** 0.25
DEEPNORM_BETA = (8 * DEPTH) ** -0.25
ATTN_SCALE = HEAD_DIM ** -0.5
LN_EPS = 1e-5
RMS_EPS = 1e-6
F32 = jnp.float32

kernel_name = 'hybrid_gdn_dilated_s5_swa_peer_step'


def layer_norm(x, g, b):
    xf = x.astype(F32)
    mu = jnp.mean(xf, axis=-1, keepdims=True)
    var = jnp.mean(jnp.square(xf - mu), axis=-1, keepdims=True)
    return ((xf - mu) * lax.rsqrt(var + LN_EPS) * g.astype(F32) + b.astype(F32)).astype(x.dtype)


def l2_normalize(x):
    return x * lax.rsqrt(jnp.sum(jnp.square(x), axis=-1, keepdims=True) + RMS_EPS)


def split_offsets(sizes):
    offs, acc = [], 0
    for s in sizes[:-1]:
        acc += s
        offs.append(acc)
    return offs


def attend_stats(q, k, v, mask):
    s = jnp.einsum('...qhgd,...khd->...hgqk', q.astype(F32), k.astype(F32)) * ATTN_SCALE
    s = jnp.where(mask, s, -jnp.inf)
    m = jnp.max(s, axis=-1)
    p = jnp.exp(s - m[..., None])
    l = jnp.sum(p, axis=-1)
    acc = jnp.einsum('...hgqk,...khd->...qhgd', p, v.astype(F32))
    return acc, jnp.moveaxis(m, -1, -3), jnp.moveaxis(l, -1, -3)


def banded_stats(q, k, v, max_dist):
    Bsz, L = q.shape[:2]
    pad = (-L) % BAND
    padl = lambda t: jnp.pad(t, ((0, 0), (0, pad)) + ((0, 0),) * (t.ndim - 2))
    q, k, v = padl(q), padl(k), padl(v)
    nb = (L + pad) // BAND

    def two_blocks(t):
        cur = t.reshape((Bsz, nb, BAND) + t.shape[2:])
        prev = jnp.pad(cur, ((0, 0), (1, 0)) + ((0, 0),) * (cur.ndim - 2))[:, :-1]
        return jnp.concatenate([prev, cur], axis=2)

    qb = q.reshape((Bsz, nb, BAND) + q.shape[2:])
    qi = jnp.arange(BAND)[:, None]
    kj = jnp.arange(2 * BAND)[None, :]
    dist = BAND + qi - kj
    blk = jnp.arange(nb)[:, None, None]
    mask = (dist >= 0) & (dist <= max_dist) & ((blk > 0) | (kj >= BAND))
    acc, m, l = attend_stats(qb, two_blocks(k), two_blocks(v), mask[:, None, None])
    unblock = lambda t: t.reshape((Bsz, nb * BAND) + t.shape[3:])[:, :L]
    return unblock(acc), unblock(m), unblock(l)


def merge_by_denominator(stats):
    m_max = jnp.max(jnp.stack([m for _, m, _ in stats]), axis=0)
    num = None
    den = None
    for acc, m, l in stats:
        wgt = jnp.exp(m - m_max)
        num = acc * wgt[..., None] if num is None else num + acc * wgt[..., None]
        den = l * wgt if den is None else den + l * wgt
    return num / den[..., None]


def sink_normalise(acc, m, l, sinks):
    sk = sinks.astype(F32)
    mm = jnp.maximum(m, sk)
    scale = jnp.exp(m - mm)
    den = l * scale + jnp.exp(sk - mm)
    return acc * (scale / den)[..., None]


def gated_delta_rule(q, k, v, log_decay, beta, s0):
    Bsz, L, H, _ = q.shape
    C = GDN_CHUNK
    pad = (-L) % C
    n_chunks = (L + pad) // C

    def prep(t):
        t = jnp.pad(t, ((0, 0), (0, pad)) + ((0, 0),) * (t.ndim - 2))
        t = t.reshape((Bsz, n_chunks, C) + t.shape[2:])
        return jnp.moveaxis(t, 3, 2)

    q, k, v, g, beta = prep(q), prep(k), prep(v), prep(log_decay), prep(beta)
    g = jnp.cumsum(g, axis=-1)
    idx = jnp.arange(C)
    causal = idx[:, None] >= idx[None, :]
    strict = idx[:, None] > idx[None, :]
    decay = jnp.exp(jnp.where(causal, g[..., :, None] - g[..., None, :], -jnp.inf))
    kb = k * beta[..., None]
    lmat = jnp.where(strict, jnp.einsum('bnhid,bnhjd->bnhij', kb, k), 0.0) * decay
    rhs = jnp.concatenate([v * beta[..., None], kb * jnp.exp(g)[..., None]], axis=-1)
    eye = jnp.eye(C, dtype=F32)
    sol = lax.linalg.triangular_solve(eye + lmat, rhs, left_side=True, lower=True, unit_diagonal=True)
    u, w = sol[..., :A_DV], sol[..., A_DV:]
    qk = jnp.einsum('bnhid,bnhjd->bnhij', q, k) * decay
    q_dec = q * jnp.exp(g)[..., None]
    k_dec = k * jnp.exp(g[..., -1:] - g)[..., None]
    g_last = jnp.exp(g[..., -1])

    def step(s, xs):
        u_n, w_n, qk_n, qd_n, kd_n, gl_n = xs
        v_new = u_n - jnp.einsum('bhck,bhkv->bhcv', w_n, s)
        o = jnp.einsum('bhck,bhkv->bhcv', qd_n, s) + jnp.einsum('bhij,bhjv->bhiv', qk_n, v_new)
        s = s * gl_n[..., None, None] + jnp.einsum('bhck,bhcv->bhkv', kd_n, v_new)
        return s, o

    xs = tuple(jnp.moveaxis(t, 1, 0) for t in (u, w, qk, q_dec, k_dec, g_last))
    s_fin, o = lax.scan(step, s0, xs)
    o = jnp.swapaxes(jnp.moveaxis(o, 0, 1), 2, 3).reshape(Bsz, n_chunks * C, H, A_DV)[:, :L]
    return o, s_fin


def gdn_mixer(qkv, z, alpha_in, beta_in, conv_w, a_log, dt_bias, norm_w, conv_buf, s0):
    Bsz, L, _ = qkv.shape
    if conv_buf is None:
        conv_buf = jnp.zeros((Bsz, CONV_W - 1, A_CONV_CH), qkv.dtype)
    ext = jnp.concatenate([conv_buf.astype(qkv.dtype), qkv], axis=1)
    conv = ext[:, 0:L].astype(F32) * conv_w[0].astype(F32)
    for j in range(1, CONV_W):
        conv = conv + ext[:, j:j + L].astype(F32) * conv_w[j].astype(F32)
    new_buf = ext[:, L:]
    q, k, v = jnp.split(jax.nn.silu(conv), [A_QK, 2 * A_QK], axis=-1)
    q = l2_normalize(q.reshape(Bsz, L, A_HEADS, A_DK)) * (A_DK ** -0.5)
    k = l2_normalize(k.reshape(Bsz, L, A_HEADS, A_DK))
    v = v.reshape(Bsz, L, A_HEADS, A_DV)
    beta = jax.nn.sigmoid(beta_in.astype(F32))
    log_decay = -jnp.exp(a_log.astype(F32)) * jax.nn.softplus(alpha_in.astype(F32) + dt_bias.astype(F32))
    if s0 is None:
        s0 = jnp.zeros((Bsz, A_HEADS, A_DK, A_DV), F32)
    o, s_new = gated_delta_rule(q, k, v, log_decay, beta, s0.astype(F32))
    o = o * lax.rsqrt(jnp.mean(jnp.square(o), axis=-1, keepdims=True) + RMS_EPS) * norm_w.astype(F32)
    o = o * jax.nn.silu(z.astype(F32).reshape(Bsz, L, A_HEADS, A_DV))
    return o.reshape(Bsz, L, A_VW), new_buf, s_new


def dilated_prompt(q, k, v):
    Bsz, L = q.shape[:2]
    stats = []
    for window, dil in B_BRANCHES:
        ls = L // dil

        def to_sub(t):
            t = t.reshape((Bsz, ls, dil) + t.shape[2:])
            return jnp.swapaxes(t, 1, 2).reshape((Bsz * dil, ls) + t.shape[3:])

        def from_sub(t):
            t = t.reshape((Bsz, dil, ls) + t.shape[2:])
            return jnp.swapaxes(t, 1, 2).reshape((Bsz, L) + t.shape[3:])

        acc, m, l = banded_stats(to_sub(q)[:, :, :, None], to_sub(k), to_sub(v), window // dil)
        stats.append((from_sub(acc), from_sub(m), from_sub(l)))
    return merge_by_denominator(stats)[:, :, :, 0]


def dilated_sample(q, k, v, k_buf, v_buf):
    T = q.shape[1]
    lw = k_buf.shape[1]
    kc = jnp.concatenate([k_buf.astype(k.dtype), k], axis=1)
    vc = jnp.concatenate([v_buf.astype(v.dtype), v], axis=1)
    qq = q[:, :, None, :, None, :]
    stats = []
    for window, dil in B_BRANCHES:
        idx = lw + jnp.arange(T)[:, None] - dil * jnp.arange(window // dil + 1)[None, :]
        valid = idx >= 0
        idx = jnp.maximum(idx, 0)
        acc, m, l = attend_stats(qq, kc[:, idx], vc[:, idx], valid[:, None, None, None, :])
        stats.append((acc[:, :, 0], m[:, :, 0], l[:, :, 0]))
    return merge_by_denominator(stats)[:, :, :, 0]


def ssm_combine(e1, e2):
    a1, b1 = e1
    a2, b2 = e2
    return a1 * a2, a2 * b1 + b2


def s5_mixer(u, lam_re, lam_im, log_step, b_re, b_im, c_re, c_im, d_skip, glu_w, glu_b, h0):
    Bsz, L, _ = u.shape
    lam = lax.complex(lam_re.astype(F32), lam_im.astype(F32))
    lam_bar = jnp.exp(lam * jnp.exp(log_step.astype(F32))[:, None])
    b_bar = ((lam_bar - 1.0) / lam)[:, :, None] * lax.complex(b_re.astype(F32), b_im.astype(F32))
    c_mat = lax.complex(c_re.astype(F32), c_im.astype(F32))
    uf = u.astype(F32)
    bu = jnp.einsum('gpc,blgc->blgp', b_bar, uf.reshape(Bsz, L, C_GROUPS, C_GROUP).astype(jnp.complex64))
    if h0 is not None:
        bu = bu.at[:, 0].add(lam_bar * h0)
    a = jnp.broadcast_to(lam_bar, bu.shape)
    _, h = lax.associative_scan(ssm_combine, (a, bu), axis=1)
    y = jnp.einsum('gcp,blgp->blgc', c_mat, h).real.reshape(Bsz, L, C_WIDTH) + d_skip.astype(F32) * uf
    val, gate = jnp.split(y @ glu_w.astype(F32) + glu_b.astype(F32), 2, axis=-1)
    return val * jax.nn.sigmoid(gate), h[:, -1]


def peer_ffn(x, w_query, sub_keys, u_tab, v_tab):
    Bsz, L, D = x.shape
    t = x.reshape(Bsz * L, D)
    n = t.shape[0]
    q = (t @ w_query).reshape(n, PEER_HEADS, 2, PEER_HALF)
    s = jnp.einsum('nhcd,hckd->nhck', q.astype(F32), sub_keys.astype(F32))
    v1, i1 = lax.top_k(s[:, :, 0], PEER_TOPK)
    v2, i2 = lax.top_k(s[:, :, 1], PEER_TOPK)
    cand = (v1[..., :, None] + v2[..., None, :]).reshape(n, PEER_HEADS, PEER_TOPK * PEER_TOPK)
    sc, ci = lax.top_k(cand, PEER_TOPK)
    expert = (jnp.take_along_axis(i1, ci // PEER_TOPK, axis=-1) * N_KEYS
              + jnp.take_along_axis(i2, ci % PEER_TOPK, axis=-1))
    gate = jax.nn.softmax(sc, axis=-1)
    pad = (-n) % PEER_BLOCK
    tb = jnp.pad(t, ((0, pad), (0, 0))).reshape(-1, PEER_BLOCK, D)
    eb = jnp.pad(expert, ((0, pad), (0, 0), (0, 0))).reshape(-1, PEER_BLOCK, PEER_HEADS, PEER_TOPK)
    gb = jnp.pad(gate, ((0, pad), (0, 0), (0, 0))).reshape(-1, PEER_BLOCK, PEER_HEADS, PEER_TOPK)

    def expert_block(args):
        tt, ee, gg = args
        h = jnp.einsum('tjkd,td->tjk', u_tab[ee], tt)
        coef = gg * jax.nn.gelu(h.astype(F32), approximate=False)
        return jnp.einsum('tjk,tjkd->td', coef.astype(tt.dtype), v_tab[ee])

    out = lax.map(expert_block, (tb, eb, gb)).reshape(-1, D)[:n]
    return out.reshape(Bsz, L, D)


def layer_forward(x, w, cache):
    Bsz, L, _ = x.shape
    fresh = cache is None
    proj = jnp.einsum('bld,dc->blc', x, w['w_in'])
    a_qkv, a_z, a_alpha, a_beta, b_qkv, c_u, d_q, d_kp, d_vp = jnp.split(proj, split_offsets(IN_SIZES), axis=-1)

    o_a, a_conv_new, a_state_new = gdn_mixer(
        a_qkv, a_z, a_alpha, a_beta, w['a_conv_w'], w['a_log'], w['a_dt_bias'], w['a_norm_w'],
        None if fresh else cache['a_conv'], None if fresh else cache['a_state'])

    b_q, b_k, b_v = [t.reshape(Bsz, L, B_HEADS, HEAD_DIM) for t in jnp.split(b_qkv, 3, axis=-1)]
    if fresh:
        o_b = dilated_prompt(b_q, b_k, b_v)
        keep = min(B_CACHE, L)
        b_k_new, b_v_new = b_k[:, L - keep:], b_v[:, L - keep:]
    else:
        o_b = dilated_sample(b_q, b_k, b_v, cache['b_k'], cache['b_v'])
        b_k_new, b_v_new = b_k, b_v

    h0 = None if fresh else lax.complex(cache['c_re'].astype(F32), cache['c_im'].astype(F32))
    o_c, c_state = s5_mixer(c_u, w['c_lambda_re'], w['c_lambda_im'], w['c_log_step'], w['c_b_re'], w['c_b_im'],
                            w['c_c_re'], w['c_c_im'], w['c_d'], w['c_glu_w'], w['c_glu_b'], h0)

    d_q = d_q.reshape(Bsz, L, D_KV_HEADS, D_GROUP, HEAD_DIM)
    d_k = d_kp.reshape(Bsz, L, D_KV_HEADS, HEAD_DIM)
    d_v = d_vp.reshape(Bsz, L, D_KV_HEADS, HEAD_DIM)
    if fresh:
        acc, m, l = banded_stats(d_q, d_k, d_v, D_WINDOW - 1)
        keep = min(D_WINDOW, L)
        d_k_new, d_v_new = d_k[:, L - keep:], d_v[:, L - keep:]
    else:
        lw = cache['d_k'].shape[1]
        kc = jnp.concatenate([cache['d_k'].astype(d_k.dtype), d_k], axis=1)
        vc = jnp.concatenate([cache['d_v'].astype(d_v.dtype), d_v], axis=1)
        dist = lw + jnp.arange(L)[:, None] - jnp.arange(lw + L)[None, :]
        acc, m, l = attend_stats(d_q, kc, vc, (dist >= 0) & (dist < D_WINDOW))
        d_k_new, d_v_new = d_k, d_v
    o_d = sink_normalise(acc, m, l, w['d_sinks'])

    mix = jnp.concatenate([o_a.astype(x.dtype), o_b.reshape(Bsz, L, B_WIDTH).astype(x.dtype),
                           o_c.astype(x.dtype), o_d.reshape(Bsz, L, D_WIDTH).astype(x.dtype)], axis=-1)
    x = layer_norm(DEEPNORM_ALPHA * x + mix @ w['w_out'], w['ln1_g'], w['ln1_b'])
    x = layer_norm(DEEPNORM_ALPHA * x + peer_ffn(x, w['peer_wq'], w['peer_sub_keys'], w['peer_u'], w['peer_v']),
                   w['ln2_g'], w['ln2_b'])
    return x, (a_conv_new, a_state_new, b_k_new, b_v_new, jnp.real(c_state), jnp.imag(c_state), d_k_new, d_v_new)


def setup_inputs(seed: int = 0) -> dict:
    key = jax.random.key(seed)
    keys = iter(jax.random.split(key, 48))

    def normal(shape, scale):
        return jax.random.normal(next(keys), shape, F32) * scale

    def uniform(shape, lo, hi):
        return jax.random.uniform(next(keys), shape, F32, lo, hi)

    lb = min(B_CACHE, PAST_LEN)
    ld = min(D_WINDOW, PAST_LEN)
    beta = DEEPNORM_BETA
    dt = jnp.exp(uniform((DEPTH, A_HEADS), math.log(1e-3), math.log(1e-1)))
    return {
        'x_prompt': normal((BATCH, SEQ, D_MODEL), 1.0),
        'x_sample': normal((DEC_BATCH, DEC_SEQ, D_MODEL), 1.0),
        'cache_a_conv': normal((DEPTH, DEC_BATCH, CONV_W - 1, A_CONV_CH), 1.0),
        'state_a': normal((DEPTH, DEC_BATCH, A_HEADS, A_DK, A_DV), 0.1),
        'cache_b_k': normal((DEPTH, DEC_BATCH, lb, B_HEADS, HEAD_DIM), 1.0),
        'cache_b_v': normal((DEPTH, DEC_BATCH, lb, B_HEADS, HEAD_DIM), 1.0),
        'state_c_re': normal((DEPTH, DEC_BATCH, C_GROUPS, C_STATE), 0.3),
        'state_c_im': normal((DEPTH, DEC_BATCH, C_GROUPS, C_STATE), 0.3),
        'cache_d_k': normal((DEPTH, DEC_BATCH, ld, D_KV_HEADS, HEAD_DIM), 1.0),
        'cache_d_v': normal((DEPTH, DEC_BATCH, ld, D_KV_HEADS, HEAD_DIM), 1.0),
        'w_in': normal((DEPTH, D_MODEL, IN_COLS), D_MODEL ** -0.5),
        'a_conv_w': normal((DEPTH, CONV_W, A_CONV_CH), CONV_W ** -0.5),
        'a_log': jnp.log(uniform((DEPTH, A_HEADS), 1.0, 16.0)),
        'a_dt_bias': dt + jnp.log(-jnp.expm1(-dt)),
        'a_norm_w': 1.0 + normal((DEPTH, A_DV), 0.02),
        'c_lambda_re': -0.5 + normal((DEPTH, C_GROUPS, C_STATE), 0.01),
        'c_lambda_im': math.pi * jnp.arange(C_STATE, dtype=F32) + normal((DEPTH, C_GROUPS, C_STATE), 0.01),
        'c_log_step': uniform((DEPTH, C_GROUPS), math.log(1e-3), math.log(1e-1)),
        'c_b_re': normal((DEPTH, C_GROUPS, C_STATE, C_GROUP), (2 * C_GROUP) ** -0.5),
        'c_b_im': normal((DEPTH, C_GROUPS, C_STATE, C_GROUP), (2 * C_GROUP) ** -0.5),
        'c_c_re': normal((DEPTH, C_GROUPS, C_GROUP, C_STATE), (2 * C_STATE) ** -0.5),
        'c_c_im': normal((DEPTH, C_GROUPS, C_GROUP, C_STATE), (2 * C_STATE) ** -0.5),
        'c_d': normal((DEPTH, C_WIDTH), 1.0),
        'c_glu_w': normal((DEPTH, C_WIDTH, 2 * C_WIDTH), C_WIDTH ** -0.5),
        'c_glu_b': normal((DEPTH, 2 * C_WIDTH), 0.02),
        'd_sinks': normal((DEPTH, D_KV_HEADS, D_GROUP), 0.5),
        'w_out': normal((DEPTH, MIX_WIDTH, D_MODEL), beta * MIX_WIDTH ** -0.5),
        'ln1_g': 1.0 + normal((DEPTH, D_MODEL), 0.02),
        'ln1_b': normal((DEPTH, D_MODEL), 0.02),
        'peer_wq': normal((DEPTH, D_MODEL, PEER_HEADS * PEER_QDIM), D_MODEL ** -0.5),
        'peer_sub_keys': normal((DEPTH, PEER_HEADS, 2, N_KEYS, PEER_HALF), PEER_HALF ** -0.5),
        'peer_u': normal((DEPTH, N_EXPERTS, D_MODEL), D_MODEL ** -0.5),
        'peer_v': normal((DEPTH, N_EXPERTS, D_MODEL), beta * 0.5),
        'ln2_g': 1.0 + normal((DEPTH, D_MODEL), 0.02),
        'ln2_b': normal((DEPTH, D_MODEL), 0.02),
    }


def reference(x_prompt, x_sample, cache_a_conv, state_a, cache_b_k, cache_b_v, state_c_re, state_c_im,
              cache_d_k, cache_d_v, w_in, a_conv_w, a_log, a_dt_bias, a_norm_w, c_lambda_re, c_lambda_im,
              c_log_step, c_b_re, c_b_im, c_c_re, c_c_im, c_d, c_glu_w, c_glu_b, d_sinks, w_out, ln1_g, ln1_b,
              peer_wq, peer_sub_keys, peer_u, peer_v, ln2_g, ln2_b):
    xp = x_prompt
    xs = x_sample
    p_states = []
    s_states = []
    for l in range(DEPTH):
        w = {
            'w_in': w_in[l], 'a_conv_w': a_conv_w[l], 'a_log': a_log[l], 'a_dt_bias': a_dt_bias[l],
            'a_norm_w': a_norm_w[l], 'c_lambda_re': c_lambda_re[l], 'c_lambda_im': c_lambda_im[l],
            'c_log_step': c_log_step[l], 'c_b_re': c_b_re[l], 'c_b_im': c_b_im[l], 'c_c_re': c_c_re[l],
            'c_c_im': c_c_im[l], 'c_d': c_d[l], 'c_glu_w': c_glu_w[l], 'c_glu_b': c_glu_b[l],
            'd_sinks': d_sinks[l], 'w_out': w_out[l], 'ln1_g': ln1_g[l], 'ln1_b': ln1_b[l],
            'peer_wq': peer_wq[l], 'peer_sub_keys': peer_sub_keys[l], 'peer_u': peer_u[l], 'peer_v': peer_v[l],
            'ln2_g': ln2_g[l], 'ln2_b': ln2_b[l],
        }
        cache = {
            'a_conv': cache_a_conv[l], 'a_state': state_a[l], 'b_k': cache_b_k[l], 'b_v': cache_b_v[l],
            'c_re': state_c_re[l], 'c_im': state_c_im[l], 'd_k': cache_d_k[l], 'd_v': cache_d_v[l],
        }
        xp, st_p = layer_forward(xp, w, None)
        xs, st_s = layer_forward(xs, w, cache)
        p_states.append(st_p)
        s_states.append(st_s)
    p_a_conv, p_a_state, p_b_k, p_b_v, p_c_re, p_c_im, p_d_k, p_d_v = [
        jnp.stack([st[i] for st in p_states]) for i in range(8)]
    s_a_conv, s_a_state, s_b_k, s_b_v, s_c_re, s_c_im, s_d_k, s_d_v = [
        jnp.stack([st[i] for st in s_states]) for i in range(8)]
    return (xp, xs, p_a_conv, p_a_state, p_b_k, p_b_v, p_c_re, p_c_im, p_d_k, p_d_v,
            s_a_conv, s_a_state, s_b_k, s_b_v, s_c_re, s_c_im, s_d_k, s_d_v)
```

```python
import functools
import math

import jax
import jax.numpy as jnp
from jax import lax
from jax.experimental import pallas as pl
from jax.experimental.pallas import tpu as pltpu

F32 = jnp.float32
BF16 = jnp.bfloat16
HIGHEST = lax.Precision.HIGHEST

D_MODEL = 2048
BATCH = 4
SEQ = 2048
DEPTH = 4
DEC_BATCH = 32
DEC_SEQ = 8
HEAD_DIM = 64
A_HEADS = 4
A_DK = 128
A_QK = A_HEADS * A_DK
A_VW = A_HEADS * A_DK
A_CONV_CH = 2 * A_QK + A_VW
CONV_W = 4
B_HEADS = 8
B_WIDTH = B_HEADS * HEAD_DIM
B_CACHE = 2048
C_WIDTH = 512
C_GROUP = 16
C_GROUPS = C_WIDTH // C_GROUP
C_STATE = 64
D_HEADS = 8
D_KV_HEADS = 2
D_WIDTH = D_HEADS * HEAD_DIM
D_KV_WIDTH = D_KV_HEADS * HEAD_DIM
D_WINDOW = 128
PEER_HEADS = 8
N_KEYS = 128
N_EXPERTS = N_KEYS * N_KEYS
PEER_TOPK = 16
DEEPNORM_ALPHA = (2 * DEPTH) ** 0.25
ATTN_SCALE = HEAD_DIM ** -0.5
LN_EPS = 1e-5
RMS_EPS = 1e-6

N_PROMPT = BATCH * SEQ
N_SAMPLE = DEC_BATCH * DEC_SEQ
N_TOK = N_PROMPT + N_SAMPLE

SUBLANES = 8
LANES = 128
VMEM_LIMIT = 56 * 1024 * 1024

COL_A_QKV = 0
COL_A_Z = 1536
COL_B_Q = 2048
COL_B_K = 2560
COL_B_V = 3072
COL_C_U = 3584
COL_D_Q = 4096
COL_D_K = 4608
COL_D_V = 4736
COL_A_AB = 4864
PROJ_COLS = 4992

NEG_BIG = -1e30


def _silu(x):
    return x * jax.nn.sigmoid(x)


def _softplus(x):
    return jnp.maximum(x, 0.0) + jnp.log1p(jnp.exp(-jnp.abs(x)))


def _dot(a, b, precision=None):
    return jnp.dot(a, b, preferred_element_type=F32, precision=precision)


def _dot_nt(a, b, precision=None):
    return lax.dot_general(a, b, (((1,), (1,)), ((), ())), preferred_element_type=F32, precision=precision)


def _layer_norm(x, g, b):
    mu = jnp.mean(x, axis=-1, keepdims=True)
    xc = x - mu
    var = jnp.mean(xc * xc, axis=-1, keepdims=True)
    return xc * lax.rsqrt(var + LN_EPS) * g + b


def _params(*sem):
    return pltpu.CompilerParams(dimension_semantics=sem, vmem_limit_bytes=VMEM_LIMIT)


def _in_proj_kernel(x_ref, w_ref, o_ref, xb_ref):
    @pl.when(pl.program_id(1) == 0)
    def _():
        xb_ref[...] = x_ref[...].astype(BF16)

    o_ref[...] = _dot(xb_ref[...], w_ref[...])


def in_proj(x, w_all, layer, tm=768, tn=1664):
    n, k = x.shape
    m = w_all.shape[2]
    return pl.pallas_call(
        _in_proj_kernel,
        grid=(n // tm, m // tn),
        in_specs=[pl.BlockSpec((tm, k), lambda i, j: (i, 0)),
                  pl.BlockSpec((None, k, tn), lambda i, j: (layer, 0, j))],
        out_specs=pl.BlockSpec((tm, tn), lambda i, j: (i, j)),
        out_shape=jax.ShapeDtypeStruct((n, m), F32),
        scratch_shapes=[pltpu.VMEM((tm, k), BF16)],
        compiler_params=_params("parallel", "arbitrary"),
        name="in_proj",
    )(x, w_all)


def _unit_lower_inverse(lmat, c):
    blk = min(16, c)
    row = lax.broadcasted_iota(jnp.int32, (c, c), 0)
    col = lax.broadcasted_iota(jnp.int32, (c, c), 1)
    eye = (row == col).astype(F32)
    if c > blk:
        same = (row // blk) == (col // blk)
        ld = jnp.where(same, lmat, 0.0)
        lo = jnp.where(same, 0.0, lmat)
    else:
        ld, lo = lmat, None
    td = eye - ld
    pw = ld
    p = 1
    while 2 * p < blk:
        pw = _dot(pw, pw, HIGHEST)
        p *= 2
        td = td + _dot(td, pw, HIGHEST)
    if lo is None:
        return td
    assert c // blk == 4
    mm = _dot(td, lo, HIGHEST)
    m2 = _dot(mm, mm, HIGHEST)
    y = td - _dot(mm, td, HIGHEST)
    return y + _dot(m2, y, HIGHEST)


def _gdn_kernel(qkv_ref, z_ref, ab_ref, cw_ref, alog_ref, dtb_ref, nw_ref, cbuf_ref, s0_ref,
                o_ref, sfin_ref, ext_ref, s_ref, *, c):
    n = pl.program_id(1)

    @pl.when(n == 0)
    def _():
        ext_ref[0:SUBLANES, :] = cbuf_ref[...]
        s_ref[...] = s0_ref[...]

    ext_ref[SUBLANES:SUBLANES + c, :] = qkv_ref[...]
    cw = cw_ref[...]
    first = SUBLANES - (CONV_W - 1)
    conv = ext_ref[pl.ds(first, c), :] * cw[0:1, :]
    for j in range(1, CONV_W):
        conv = conv + ext_ref[pl.ds(first + j, c), :] * cw[j:j + 1, :]
    carry = ext_ref[c:c + SUBLANES, :]
    ext_ref[0:SUBLANES, :] = carry
    conv = _silu(conv)

    ab = ab_ref[...]
    g = -jnp.exp(alog_ref[...]) * _softplus(ab + dtb_ref[...])
    beta = jax.nn.sigmoid(ab)
    row = lax.broadcasted_iota(jnp.int32, (c, c), 0)
    col = lax.broadcasted_iota(jnp.int32, (c, c), 1)
    causal = row >= col
    strict = row > col
    gc = _dot(causal.astype(F32), g, HIGHEST)
    gct = gc.T
    z = z_ref[...]
    nw = nw_ref[...]

    for h in range(A_HEADS):
        q = conv[:, h * A_DK:(h + 1) * A_DK]
        k = conv[:, A_QK + h * A_DK:A_QK + (h + 1) * A_DK]
        v = conv[:, 2 * A_QK + h * A_DK:2 * A_QK + (h + 1) * A_DK]
        q = q * lax.rsqrt(jnp.sum(q * q, axis=-1, keepdims=True) + RMS_EPS) * (A_DK ** -0.5)
        k = k * lax.rsqrt(jnp.sum(k * k, axis=-1, keepdims=True) + RMS_EPS)
        bcol = beta[:, A_HEADS + h:A_HEADS + h + 1]
        gcol = gc[:, h:h + 1]
        grow = gct[h:h + 1, :]
        decay = jnp.where(causal, jnp.exp(jnp.where(causal, gcol - grow, 0.0)), 0.0)
        kb = k * bcol
        lmat = jnp.where(strict, _dot_nt(kb, k, HIGHEST) * decay, 0.0)
        tinv = _unit_lower_inverse(lmat, c)
        eg = jnp.exp(gcol)
        u = _dot(tinv, v * bcol, HIGHEST)
        w = _dot(tinv, kb * eg, HIGHEST)
        qk = _dot_nt(q, k, HIGHEST) * decay
        glast = gc[c - 1:c, h:h + 1]
        kd = k * jnp.exp(glast - gcol)
        s = s_ref[h]
        v_new = u - _dot(w, s, HIGHEST)
        o = _dot(q * eg, s, HIGHEST) + _dot(qk, v_new, HIGHEST)
        s_ref[h] = s * jnp.exp(glast) + _dot(kd.T, v_new, HIGHEST)
        o = o * lax.rsqrt(jnp.mean(o * o, axis=-1, keepdims=True) + RMS_EPS) * nw
        o_ref[:, h * A_DK:(h + 1) * A_DK] = o * _silu(z[:, h * A_DK:(h + 1) * A_DK])

    @pl.when(n == pl.num_programs(1) - 1)
    def _():
        sfin_ref[...] = s_ref[...]


def gdn_mixer(proj, row0, nb, seq, c, conv_w, a_log, dt_bias, norm_w, conv_buf, s0, layer):
    nchunks = seq // c
    rb0 = row0 // c
    kern = functools.partial(_gdn_kernel, c=c)
    return pl.pallas_call(
        kern,
        grid=(nb, nchunks),
        in_specs=[
            pl.BlockSpec((c, A_CONV_CH), lambda b, n: (rb0 + b * nchunks + n, COL_A_QKV // A_CONV_CH)),
            pl.BlockSpec((c, A_VW), lambda b, n: (rb0 + b * nchunks + n, COL_A_Z // A_VW)),
            pl.BlockSpec((c, LANES), lambda b, n: (rb0 + b * nchunks + n, COL_A_AB // LANES)),
            pl.BlockSpec((None, CONV_W, A_CONV_CH), lambda b, n: (layer, 0, 0)),
            pl.BlockSpec((None, 1, LANES), lambda b, n: (layer, 0, 0)),
            pl.BlockSpec((None, 1, LANES), lambda b, n: (layer, 0, 0)),
            pl.BlockSpec((None, 1, A_DK), lambda b, n: (layer, 0, 0)),
            pl.BlockSpec((None, SUBLANES, A_CONV_CH), lambda b, n: (b, 0, 0)),
            pl.BlockSpec((None, A_HEADS, A_DK, A_DK), lambda b, n: (b, 0, 0, 0)),
        ],
        out_specs=[
            pl.BlockSpec((c, A_VW), lambda b, n: (b * nchunks + n, 0)),
            pl.BlockSpec((None, A_HEADS, A_DK, A_DK), lambda b, n: (b, 0, 0, 0)),
        ],
        out_shape=[jax.ShapeDtypeStruct((nb * seq, A_VW), F32),
                   jax.ShapeDtypeStruct((nb, A_HEADS, A_DK, A_DK), F32)],
        scratch_shapes=[pltpu.VMEM((c + SUBLANES, A_CONV_CH), F32),
                        pltpu.VMEM((A_HEADS, A_DK, A_DK), F32)],
        compiler_params=_params("parallel", "arbitrary"),
        name=f"gdn_c{c}",
    )(proj, proj, proj, conv_w, a_log, dt_bias, norm_w, conv_buf, s0)


def _multiplicity_dilated(dist):
    nonneg = dist >= 0
    c1 = nonneg & (dist <= 128)
    c2 = nonneg & (dist <= 512) & ((dist & 3) == 0)
    c3 = nonneg & (dist <= 2048) & ((dist & 15) == 0)
    return c1.astype(F32) + c2.astype(F32) + c3.astype(F32)


def _multiplicity_window(dist):
    return ((dist >= 0) & (dist < D_WINDOW)).astype(F32)


def _online_update(s, mult, v_blk, m_old, l_old, acc_old):
    s = jnp.where(mult > 0.0, s, NEG_BIG)
    m_new = jnp.maximum(m_old, jnp.max(s, axis=-1, keepdims=True))
    alpha = jnp.exp(m_old - m_new)
    p = jnp.exp(s - m_new) * mult
    l_new = alpha * l_old + jnp.sum(p, axis=-1, keepdims=True)
    acc_new = alpha * acc_old + _dot(p, v_blk)
    return m_new, l_new, acc_new


def _attn_prompt_kernel(q_ref, k_ref, v_ref, sink_ref, o_ref, m_ref, l_ref, acc_ref,
                        *, tq, tk, n_heads, group, mult_fn, back_blocks, use_sink):
    qi = pl.program_id(1)
    m_ref[...] = jnp.full(m_ref.shape, NEG_BIG, F32)
    l_ref[...] = jnp.zeros(l_ref.shape, F32)
    acc_ref[...] = jnp.zeros(acc_ref.shape, F32)
    lo = 0 if back_blocks is None else jnp.maximum(qi - back_blocks, 0)

    def body(ki, carry):
        k0 = pl.multiple_of(ki * tk, tk)
        dist = (qi * tq + lax.broadcasted_iota(jnp.int32, (tq, tk), 0)) - (
            ki * tk + lax.broadcasted_iota(jnp.int32, (tq, tk), 1))
        mult = mult_fn(dist)
        for h in range(n_heads):
            hk = h // group
            q = q_ref[:, h * HEAD_DIM:(h + 1) * HEAD_DIM]
            kb = k_ref[pl.ds(k0, tk), hk * HEAD_DIM:(hk + 1) * HEAD_DIM]
            vb = v_ref[pl.ds(k0, tk), hk * HEAD_DIM:(hk + 1) * HEAD_DIM]
            s = _dot_nt(q, kb) * ATTN_SCALE
            m_new, l_new, acc_new = _online_update(s, mult, vb, m_ref[h], l_ref[h], acc_ref[h])
            m_ref[h] = m_new
            l_ref[h] = l_new
            acc_ref[h] = acc_new
        return carry

    lax.fori_loop(lo, qi + 1, body, 0)
    for h in range(n_heads):
        if use_sink:
            sk = sink_ref[h]
            m = m_ref[h]
            mm = jnp.maximum(m, sk)
            scale = jnp.exp(m - mm)
            den = l_ref[h] * scale + jnp.exp(sk - mm)
            out = acc_ref[h] * (scale / den)
        else:
            out = acc_ref[h] / l_ref[h]
        o_ref[:, h * HEAD_DIM:(h + 1) * HEAD_DIM] = out


def attn_prompt(proj, col_q, col_k, col_v, n_heads, n_kv, mult_fn, back_blocks, sinks, tq=256, tk=256):
    qw = n_heads * HEAD_DIM
    kw = n_kv * HEAD_DIM
    nq = SEQ // tq
    use_sink = sinks is not None
    if sinks is None:
        sinks = jnp.zeros((n_heads,), F32)
    kern = functools.partial(_attn_prompt_kernel, tq=tq, tk=tk, n_heads=n_heads, group=n_heads // n_kv,
                             mult_fn=mult_fn, back_blocks=back_blocks, use_sink=use_sink)
    return pl.pallas_call(
        kern,
        grid=(BATCH, nq),
        in_specs=[
            pl.BlockSpec((tq, qw), lambda b, i: (b * nq + i, col_q // qw)),
            pl.BlockSpec((SEQ, kw), lambda b, i: (b, col_k // kw)),
            pl.BlockSpec((SEQ, kw), lambda b, i: (b, col_v // kw)),
            pl.BlockSpec(memory_space=pltpu.SMEM),
        ],
        out_specs=pl.BlockSpec((tq, qw), lambda b, i: (b * nq + i, 0)),
        out_shape=jax.ShapeDtypeStruct((N_PROMPT, qw), F32),
        scratch_shapes=[pltpu.VMEM((n_heads, tq, 1), F32), pltpu.VMEM((n_heads, tq, 1), F32),
                        pltpu.VMEM((n_heads, tq, HEAD_DIM), F32)],
        compiler_params=_params("parallel", "arbitrary"),
        name=f"attn_prompt_{n_kv}",
    )(proj, proj, proj, sinks)


def _attn_sample_kernel(q_ref, kn_ref, vn_ref, kp_ref, vp_ref, sink_ref, o_ref,
                        *, tk, n_heads, n_kv, mult_fn, use_sink):
    t = DEC_SEQ
    group = n_heads // n_kv
    rows = group * t
    past = kp_ref.shape[0]
    trow = lax.broadcasted_iota(jnp.int32, (rows, 1), 0) % t
    for hk in range(n_kv):
        q = jnp.concatenate(
            [q_ref[:, (hk * group + g) * HEAD_DIM:(hk * group + g + 1) * HEAD_DIM] for g in range(group)], axis=0)
        m = jnp.full((rows, 1), NEG_BIG, F32)
        l = jnp.zeros((rows, 1), F32)
        acc = jnp.zeros((rows, HEAD_DIM), F32)
        for kb in range(past // tk):
            kblk = kp_ref[kb * tk:(kb + 1) * tk, hk * HEAD_DIM:(hk + 1) * HEAD_DIM]
            vblk = vp_ref[kb * tk:(kb + 1) * tk, hk * HEAD_DIM:(hk + 1) * HEAD_DIM]
            dist = (past + trow) - (kb * tk + lax.broadcasted_iota(jnp.int32, (rows, tk), 1))
            s = _dot_nt(q, kblk) * ATTN_SCALE
            m, l, acc = _online_update(s, mult_fn(dist), vblk, m, l, acc)
        kblk = kn_ref[:, hk * HEAD_DIM:(hk + 1) * HEAD_DIM]
        vblk = vn_ref[:, hk * HEAD_DIM:(hk + 1) * HEAD_DIM]
        dist = trow - lax.broadcasted_iota(jnp.int32, (rows, t), 1)
        s = _dot_nt(q, kblk) * ATTN_SCALE
        m, l, acc = _online_update(s, mult_fn(dist), vblk, m, l, acc)
        for g in range(group):
            h = hk * group + g
            mg, lg, ag = m[g * t:(g + 1) * t], l[g * t:(g + 1) * t], acc[g * t:(g + 1) * t]
            if use_sink:
                sk = sink_ref[h]
                mm = jnp.maximum(mg, sk)
                scale = jnp.exp(mg - mm)
                den = lg * scale + jnp.exp(sk - mm)
                out = ag * (scale / den)
            else:
                out = ag / lg
            o_ref[:, h * HEAD_DIM:(h + 1) * HEAD_DIM] = out


def attn_sample(proj, col_q, col_k, col_v, n_heads, n_kv, mult_fn, k_cache, v_cache, layer, sinks, tk):
    qw = n_heads * HEAD_DIM
    kw = n_kv * HEAD_DIM
    past = k_cache.shape[2]
    rb0 = N_PROMPT // DEC_SEQ
    use_sink = sinks is not None
    if sinks is None:
        sinks = jnp.zeros((n_heads,), F32)
    kern = functools.partial(_attn_sample_kernel, tk=tk, n_heads=n_heads, n_kv=n_kv, mult_fn=mult_fn,
                             use_sink=use_sink)
    return pl.pallas_call(
        kern,
        grid=(DEC_BATCH,),
        in_specs=[
            pl.BlockSpec((DEC_SEQ, qw), lambda b: (rb0 + b, col_q // qw)),
            pl.BlockSpec((DEC_SEQ, kw), lambda b: (rb0 + b, col_k // kw)),
            pl.BlockSpec((DEC_SEQ, kw), lambda b: (rb0 + b, col_v // kw)),
            pl.BlockSpec((None, None, past, kw), lambda b: (layer, b, 0, 0)),
            pl.BlockSpec((None, None, past, kw), lambda b: (layer, b, 0, 0)),
            pl.BlockSpec(memory_space=pltpu.SMEM),
        ],
        out_specs=pl.BlockSpec((DEC_SEQ, qw), lambda b: (b, 0)),
        out_shape=jax.ShapeDtypeStruct((N_SAMPLE, qw), F32),
        compiler_params=_params("parallel"),
        name=f"attn_sample_{n_kv}",
    )(proj, proj, proj, k_cache, v_cache, sinks)


C_COLBLK = LANES
C_GRPBLK = C_COLBLK // C_GROUP
C_STBLK = C_GRPBLK * C_STATE


def _cmul(ar, ai, br, bi):
    return ar * br - ai * bi, ar * bi + ai * br


def _s5_kernel(u_ref, bre_ref, bim_ref, are_ref, aim_ref, cre_ref, cim_ref, d_ref, h0re_ref, h0im_ref,
               y_ref, sre_ref, sim_ref, xre_ref, xim_ref, *, streams, steps, chained):
    nlt = C_STBLK // LANES
    lt = lambda k: slice(k * LANES, (k + 1) * LANES)
    u = u_ref[...]
    xre = _dot(u, bre_ref[...], HIGHEST)
    xim = _dot(u, bim_ref[...], HIGHEST)
    for k in range(nlt):
        xre_ref[k] = xre[:, lt(k)]
        xim_ref[k] = xim[:, lt(k)]
    a_re = jnp.stack([are_ref[:, lt(k)] for k in range(nlt)])
    a_im = jnp.stack([aim_ref[:, lt(k)] for k in range(nlt)])

    def load_rows(ref, t):
        return jnp.stack([ref[k, pl.ds(t, streams, stride=steps), :] for k in range(nlt)])

    def advance(t, hr, hi):
        nr, ni = _cmul(a_re, a_im, hr, hi)
        return nr + load_rows(xre_ref, t), ni + load_rows(xim_ref, t)

    if chained:
        zero = jnp.zeros((nlt, streams, LANES), F32)
        er, ei = lax.fori_loop(0, steps, lambda t, c: advance(t, *c), (zero, zero))
        pr, pi = a_re, a_im
        p = 1
        while p < steps:
            pr, pi = _cmul(pr, pi, pr, pi)
            p *= 2
        assert p == steps
        sub = lax.broadcasted_iota(jnp.int32, (nlt, streams, LANES), 1)
        cr = jnp.zeros((nlt, 1, LANES), F32)
        ci = jnp.zeros((nlt, 1, LANES), F32)
        hr, hi = zero, zero
        for s in range(1, streams):
            nr, ni = _cmul(pr, pi, cr, ci)
            cr = nr + er[:, s - 1:s, :]
            ci = ni + ei[:, s - 1:s, :]
            hr = jnp.where(sub == s, cr, hr)
            hi = jnp.where(sub == s, ci, hi)
    else:
        hr = jnp.stack([h0re_ref[:, lt(k)] for k in range(nlt)])
        hi = jnp.stack([h0im_ref[:, lt(k)] for k in range(nlt)])

    def store_step(t, c):
        nr, ni = advance(t, *c)
        for k in range(nlt):
            xre_ref[k, pl.ds(t, streams, stride=steps), :] = nr[k]
            xim_ref[k, pl.ds(t, streams, stride=steps), :] = ni[k]
        return nr, ni

    hr, hi = lax.fori_loop(0, steps, store_step, (hr, hi))
    last = slice(streams - 1, streams) if chained else slice(0, streams)
    for k in range(nlt):
        sre_ref[:, lt(k)] = hr[k, last, :]
        sim_ref[:, lt(k)] = hi[k, last, :]
    y = d_ref[...] * u
    for k in range(nlt):
        y = y + _dot(xre_ref[k], cre_ref[lt(k), :], HIGHEST) - _dot(xim_ref[k], cim_ref[lt(k), :], HIGHEST)
    y_ref[...] = y


def s5_scan(proj, row0, nseq, streams, steps, chained, cw, h0_re, h0_im, layer):
    rows = streams * steps
    ncb = C_WIDTH // C_COLBLK
    rb0 = row0 // rows
    cb0 = COL_C_U // C_COLBLK
    srows = 1 if chained else streams
    kern = functools.partial(_s5_kernel, streams=streams, steps=steps, chained=chained)
    state_spec = pl.BlockSpec((None, srows, C_STBLK), lambda b, j: (b, 0, j))
    wspec = lambda r, c: pl.BlockSpec((None, None, r, c), lambda b, j: (layer, j, 0, 0))
    return pl.pallas_call(
        kern,
        grid=(nseq, ncb),
        in_specs=[
            pl.BlockSpec((rows, C_COLBLK), lambda b, j: (rb0 + b, cb0 + j)),
            wspec(C_COLBLK, C_STBLK), wspec(C_COLBLK, C_STBLK),
            pl.BlockSpec((None, 1, C_STBLK), lambda b, j: (layer, 0, j)),
            pl.BlockSpec((None, 1, C_STBLK), lambda b, j: (layer, 0, j)),
            wspec(C_STBLK, C_COLBLK), wspec(C_STBLK, C_COLBLK),
            pl.BlockSpec((None, 1, C_COLBLK), lambda b, j: (layer, 0, j)),
            state_spec, state_spec,
        ],
        out_specs=[pl.BlockSpec((rows, C_COLBLK), lambda b, j: (b, j)), state_spec, state_spec],
        out_shape=[jax.ShapeDtypeStruct((nseq * rows, C_WIDTH), F32),
                   jax.ShapeDtypeStruct((nseq, srows, C_GROUPS * C_STATE), F32),
                   jax.ShapeDtypeStruct((nseq, srows, C_GROUPS * C_STATE), F32)],
        scratch_shapes=[pltpu.VMEM((C_STBLK // LANES, rows, LANES), F32),
                        pltpu.VMEM((C_STBLK // LANES, rows, LANES), F32)],
        compiler_params=_params("parallel", "parallel"),
        name=f"s5_{streams}x{steps}",
    )(proj, cw["b_re"], cw["b_im"], cw["a_re"], cw["a_im"], cw["c_re"], cw["c_im"], cw["d"], h0_re, h0_im)


def s5_weights(lam_re, lam_im, log_step, b_re, b_im, c_re, c_im, d_skip):
    lam = lax.complex(lam_re.astype(F32), lam_im.astype(F32))
    lam_bar = jnp.exp(lam * jnp.exp(log_step.astype(F32))[..., None])
    b_bar = ((lam_bar - 1.0) / lam)[..., None] * lax.complex(b_re.astype(F32), b_im.astype(F32))
    c_mat = lax.complex(c_re.astype(F32), c_im.astype(F32))
    ncb = C_WIDTH // C_COLBLK
    eye = jnp.eye(C_GRPBLK, dtype=F32)

    def pack_in(t):
        t = jnp.swapaxes(t, -1, -2).reshape(DEPTH, ncb, C_GRPBLK, C_GROUP, C_STATE)
        return jnp.einsum("lbgcp,gh->lbgchp", t, eye).reshape(DEPTH, ncb, C_COLBLK, C_STBLK)

    def pack_out(t):
        t = jnp.swapaxes(t, -1, -2).reshape(DEPTH, ncb, C_GRPBLK, C_STATE, C_GROUP)
        return jnp.einsum("lbgpc,gh->lbgphc", t, eye).reshape(DEPTH, ncb, C_STBLK, C_COLBLK)

    flat = lambda t: t.reshape(DEPTH, 1, C_GROUPS * C_STATE)
    return {
        "a_re": flat(jnp.real(lam_bar)), "a_im": flat(jnp.imag(lam_bar)),
        "b_re": pack_in(jnp.real(b_bar)), "b_im": pack_in(jnp.imag(b_bar)),
        "c_re": pack_out(jnp.real(c_mat)), "c_im": pack_out(jnp.imag(c_mat)),
        "d": d_skip.astype(F32).reshape(DEPTH, 1, C_WIDTH),
    }


def _glu_kernel(y_ref, w_ref, b_ref, o_ref):
    t = _dot(y_ref[...].astype(BF16), w_ref[...]) + b_ref[...]
    o_ref[...] = t[:, :C_WIDTH] * jax.nn.sigmoid(t[:, C_WIDTH:])


def glu(y, w_all, b_all, layer, tm=768):
    n = y.shape[0]
    return pl.pallas_call(
        _glu_kernel,
        grid=(n // tm,),
        in_specs=[pl.BlockSpec((tm, C_WIDTH), lambda i: (i, 0)),
                  pl.BlockSpec((None, C_WIDTH, 2 * C_WIDTH), lambda i: (layer, 0, 0)),
                  pl.BlockSpec((None, 1, 2 * C_WIDTH), lambda i: (layer, 0, 0))],
        out_specs=pl.BlockSpec((tm, C_WIDTH), lambda i: (i, 0)),
        out_shape=jax.ShapeDtypeStruct((n, C_WIDTH), F32),
        compiler_params=_params("parallel"),
        name="glu",
    )(y, w_all, b_all)


MIX_PARTS = 4
MIX_PART_W = D_MODEL // MIX_PARTS


def _out_proj_kernel(oa_ref, ob_ref, oc_ref, od_ref, x_ref, w_ref, g_ref, b_ref, x1_ref, xt_ref):
    acc = DEEPNORM_ALPHA * x_ref[...]
    for i, o_ref in enumerate((oa_ref, ob_ref, oc_ref, od_ref)):
        acc = acc + _dot(o_ref[...].astype(BF16), w_ref[i * MIX_PART_W:(i + 1) * MIX_PART_W, :])
    y = _layer_norm(acc, g_ref[...], b_ref[...])
    x1_ref[...] = y
    xt_ref[...] = y.T.astype(BF16)


def out_proj(o_parts, x, w_all, g_all, b_all, layer, tm=256):
    n = x.shape[0]
    part = pl.BlockSpec((tm, MIX_PART_W), lambda i: (i, 0))
    vec = pl.BlockSpec((None, 1, D_MODEL), lambda i: (layer, 0, 0))
    return pl.pallas_call(
        _out_proj_kernel,
        grid=(n // tm,),
        in_specs=[part, part, part, part,
                  pl.BlockSpec((tm, D_MODEL), lambda i: (i, 0)),
                  pl.BlockSpec((None, D_MODEL, D_MODEL), lambda i: (layer, 0, 0)),
                  vec, vec],
        out_specs=[pl.BlockSpec((tm, D_MODEL), lambda i: (i, 0)),
                   pl.BlockSpec((D_MODEL, tm), lambda i: (0, i))],
        out_shape=[jax.ShapeDtypeStruct((n, D_MODEL), F32), jax.ShapeDtypeStruct((D_MODEL, n), BF16)],
        compiler_params=_params("parallel"),
        name="out_proj",
    )(*o_parts, x, w_all, g_all, b_all)


def _max2(a, b):
    if a is None:
        return b
    if b is None:
        return a
    return jnp.maximum(a, b)


def _compare_exchange(a, b):
    if a is None:
        return b, None
    if b is None:
        return a, None
    return jnp.maximum(a, b), jnp.minimum(a, b)


def _bitonic_merge_desc(x):
    x = list(x)
    n = len(x)
    stride = n // 2
    while stride >= 1:
        for i in range(n):
            if (i & stride) == 0:
                x[i], x[i + stride] = _compare_exchange(x[i], x[i + stride])
        stride //= 2
    return x


def _sort_desc(x):
    n = len(x)
    if n == 1:
        return list(x)
    a = _sort_desc(x[:n // 2])
    b = _sort_desc(x[n // 2:])
    return _bitonic_merge_desc(a + b[::-1])


def _merge_top(a, b):
    n = len(a)
    return _bitonic_merge_desc([_max2(a[r], b[n - 1 - r]) for r in range(n)])


def _router_kernel(x_ref, wq_ref, keys_ref, st_ref, tau_ref, cz_ref):
    tm = x_ref.shape[0]
    q = _dot(x_ref[...].astype(BF16), wq_ref[...])
    sub = lax.broadcasted_iota(jnp.int32, (SUBLANES, tm), 0)
    nslab = N_KEYS // SUBLANES
    assert nslab == PEER_TOPK
    tops = [[None] * PEER_TOPK, [None] * PEER_TOPK]
    for h in range(PEER_HEADS):
        for c in range(2):
            hc = 2 * h + c
            st = _dot_nt(keys_ref[hc], q[:, hc * N_KEYS:(hc + 1) * N_KEYS])
            st_ref[hc] = st
            a = _sort_desc([st[r * SUBLANES:(r + 1) * SUBLANES, :] for r in range(nslab)])
            for shift in (4, 2, 1):
                a = _merge_top(a, [pltpu.roll(t, shift, 0) for t in a])
            for r in range(PEER_TOPK):
                tops[c][r] = a[r] if h == 0 else jnp.where(sub == h, a[r], tops[c][r])
    v1, v2 = tops
    best = [v1[0] + v2[j] for j in range(PEER_TOPK)]
    for i in range(1, PEER_TOPK):
        n_i = PEER_TOPK // (i + 1)
        best = _merge_top(best, [v1[i] + v2[j] for j in range(n_i)] + [None] * (PEER_TOPK - n_i))
    m0 = best[0]
    z = jnp.ones_like(m0)
    for r in range(1, PEER_TOPK):
        z = z + jnp.exp(best[r] - m0)
    tau_ref[...] = best[PEER_TOPK - 1]
    cz_ref[...] = m0 + jnp.log(z)


def peer_router(x1, wq_all, keys_all, layer, tm=256):
    n = x1.shape[0]
    nhc = 2 * PEER_HEADS
    return pl.pallas_call(
        _router_kernel,
        grid=(n // tm,),
        in_specs=[pl.BlockSpec((tm, D_MODEL), lambda i: (i, 0)),
                  pl.BlockSpec((None, D_MODEL, nhc * N_KEYS), lambda i: (layer, 0, 0)),
                  pl.BlockSpec((None, nhc, N_KEYS, N_KEYS), lambda i: (layer, 0, 0, 0))],
        out_specs=[pl.BlockSpec((nhc, N_KEYS, tm), lambda i: (0, 0, i)),
                   pl.BlockSpec((PEER_HEADS, tm), lambda i: (0, i)),
                   pl.BlockSpec((PEER_HEADS, tm), lambda i: (0, i))],
        out_shape=[jax.ShapeDtypeStruct((nhc, N_KEYS, n), F32),
                   jax.ShapeDtypeStruct((PEER_HEADS, n), F32),
                   jax.ShapeDtypeStruct((PEER_HEADS, n), F32)],
        compiler_params=_params("parallel"),
        name="peer_router",
    )(x1, wq_all, keys_all)


PEER_TM = 768
PEER_TE = 256
PEER_ROWS = 32
PEER_LANES = 256


def _gelu(x):
    return x * (lax.erf(x * (2.0 ** -0.5)) + 1.0) * 0.5


def _peer_kernel(xt_ref, u_ref, vt_ref, st_ref, tau_ref, cz_ref, y_ref, acc_ref, ht_ref, ct_ref):
    j = pl.program_id(1)
    tm = xt_ref.shape[1]

    @pl.when(j == 0)
    def _():
        acc_ref[...] = jnp.zeros(acc_ref.shape, F32)

    ht_ref[...] = _dot(u_ref[...], xt_ref[...])
    rows_per_step = PEER_TE // N_KEYS
    for ar in range(rows_per_step):
        a = j * rows_per_step + ar
        for lc in range(tm // PEER_LANES):
            ls = slice(lc * PEER_LANES, (lc + 1) * PEER_LANES)
            s1 = [st_ref[2 * h, pl.ds(a, 1), ls] for h in range(PEER_HEADS)]
            tau = [tau_ref[h:h + 1, ls] for h in range(PEER_HEADS)]
            cz = [cz_ref[h:h + 1, ls] for h in range(PEER_HEADS)]
            for rc in range(N_KEYS // PEER_ROWS):
                rs = slice(rc * PEER_ROWS, (rc + 1) * PEER_ROWS)
                gate = jnp.zeros((PEER_ROWS, PEER_LANES), F32)
                for h in range(PEER_HEADS):
                    sm = st_ref[2 * h + 1, rs, ls] + s1[h]
                    gate = gate + jnp.where(sm >= tau[h], jnp.exp(sm - cz[h]), 0.0)
                er = slice(ar * N_KEYS + rc * PEER_ROWS, ar * N_KEYS + (rc + 1) * PEER_ROWS)
                ct_ref[er, ls] = (_gelu(ht_ref[er, ls]) * gate).astype(BF16)
    acc_ref[...] += _dot(vt_ref[...], ct_ref[...])

    @pl.when(j == pl.num_programs(1) - 1)
    def _():
        y_ref[...] = acc_ref[...].T


def peer_experts(xt, u_all, vt_all, st, tau, cz, layer):
    n = xt.shape[1]
    nhc = 2 * PEER_HEADS
    tm, te = PEER_TM, PEER_TE
    return pl.pallas_call(
        _peer_kernel,
        grid=(n // tm, N_EXPERTS // te),
        in_specs=[pl.BlockSpec((D_MODEL, tm), lambda i, j: (0, i)),
                  pl.BlockSpec((None, te, D_MODEL), lambda i, j: (layer, j, 0)),
                  pl.BlockSpec((None, D_MODEL, te), lambda i, j: (layer, 0, j)),
                  pl.BlockSpec((nhc, N_KEYS, tm), lambda i, j: (0, 0, i)),
                  pl.BlockSpec((PEER_HEADS, tm), lambda i, j: (0, i)),
                  pl.BlockSpec((PEER_HEADS, tm), lambda i, j: (0, i))],
        out_specs=pl.BlockSpec((tm, D_MODEL), lambda i, j: (i, 0)),
        out_shape=jax.ShapeDtypeStruct((n, D_MODEL), F32),
        scratch_shapes=[pltpu.VMEM((D_MODEL, tm), F32), pltpu.VMEM((te, tm), F32), pltpu.VMEM((te, tm), BF16)],
        compiler_params=_params("parallel", "arbitrary"),
        name="peer_experts",
    )(xt, u_all, vt_all, st, tau, cz)


def _residual_ln_kernel(x_ref, y_ref, g_ref, b_ref, o_ref):
    o_ref[...] = _layer_norm(DEEPNORM_ALPHA * x_ref[...] + y_ref[...], g_ref[...], b_ref[...])


def residual_ln(x, y, g_all, b_all, layer, tm=256):
    n = x.shape[0]
    row = pl.BlockSpec((tm, D_MODEL), lambda i: (i, 0))
    vec = pl.BlockSpec((None, 1, D_MODEL), lambda i: (layer, 0, 0))
    return pl.pallas_call(
        _residual_ln_kernel,
        grid=(n // tm,),
        in_specs=[row, row, vec, vec],
        out_specs=row,
        out_shape=jax.ShapeDtypeStruct((n, D_MODEL), F32),
        compiler_params=_params("parallel"),
        name="residual_ln",
    )(x, y, g_all, b_all)


def _permute_w_in(w_in):
    ab0 = A_CONV_CH + A_VW
    ab1 = ab0 + 2 * A_HEADS
    pad = PROJ_COLS - w_in.shape[2]
    return jnp.concatenate(
        [w_in[:, :, :ab0], w_in[:, :, ab1:], w_in[:, :, ab0:ab1], jnp.zeros(w_in.shape[:2] + (pad,), w_in.dtype)],
        axis=2).astype(BF16)


def kernel(x_prompt, x_sample, cache_a_conv, state_a, cache_b_k, cache_b_v, state_c_re, state_c_im, cache_d_k, cache_d_v, w_in, a_conv_w, a_log, a_dt_bias, a_norm_w, c_lambda_re, c_lambda_im, c_log_step, c_b_re, c_b_im, c_c_re, c_c_im, c_d, c_glu_w, c_glu_b, d_sinks, w_out, ln1_g, ln1_b, peer_wq, peer_sub_keys, peer_u, peer_v, ln2_g, ln2_b):
    x = jnp.concatenate([x_prompt.reshape(N_PROMPT, D_MODEL), x_sample.reshape(N_SAMPLE, D_MODEL)], axis=0)

    w_in_b = _permute_w_in(w_in)
    w_out_b = w_out.astype(BF16)
    wq_b = peer_wq.astype(BF16)
    u_b = peer_u.astype(BF16)
    vt_b = jnp.swapaxes(peer_v, 1, 2).astype(BF16)
    glu_w_b = c_glu_w.astype(BF16)
    glu_b = c_glu_b.reshape(DEPTH, 1, 2 * C_WIDTH)
    keys = peer_sub_keys.reshape(DEPTH, 2 * PEER_HEADS, N_KEYS, N_KEYS)
    cw = s5_weights(c_lambda_re, c_lambda_im, c_log_step, c_b_re, c_b_im, c_c_re, c_c_im, c_d)
    lane_pad = lambda t: jnp.pad(t.astype(F32), ((0, 0), (0, LANES - t.shape[1])))[:, None, :]
    a_log_p = lane_pad(a_log)
    a_dtb_p = lane_pad(a_dt_bias)
    a_nw = a_norm_w.astype(F32)[:, None, :]
    vec = lambda t: t.astype(F32)[:, None, :]
    ln1_g3, ln1_b3, ln2_g3, ln2_b3 = vec(ln1_g), vec(ln1_b), vec(ln2_g), vec(ln2_b)
    sinks = d_sinks.reshape(DEPTH, D_HEADS)

    conv_zero = jnp.zeros((BATCH, SUBLANES, A_CONV_CH), F32)
    s_zero = jnp.zeros((BATCH, A_HEADS, A_DK, A_DK), F32)
    conv_s = jnp.pad(cache_a_conv, ((0, 0), (0, 0), (SUBLANES - (CONV_W - 1), 0), (0, 0)))
    bk_c = cache_b_k.reshape(DEPTH, DEC_BATCH, -1, B_WIDTH)
    bv_c = cache_b_v.reshape(DEPTH, DEC_BATCH, -1, B_WIDTH)
    dk_c = cache_d_k.reshape(DEPTH, DEC_BATCH, -1, D_KV_WIDTH)
    dv_c = cache_d_v.reshape(DEPTH, DEC_BATCH, -1, D_KV_WIDTH)
    hc_re = state_c_re.reshape(DEPTH, 1, DEC_BATCH, C_GROUPS * C_STATE)
    hc_im = state_c_im.reshape(DEPTH, 1, DEC_BATCH, C_GROUPS * C_STATE)
    h_zero = jnp.zeros((BATCH, 1, C_GROUPS * C_STATE), F32)

    p_states = [[] for _ in range(8)]
    s_states = [[] for _ in range(8)]
    for l in range(DEPTH):
        proj = in_proj(x, w_in_b, l)
        pp = proj[:N_PROMPT].reshape(BATCH, SEQ, PROJ_COLS)
        ps = proj[N_PROMPT:].reshape(DEC_BATCH, DEC_SEQ, PROJ_COLS)

        oa_p, sa_p = gdn_mixer(proj, 0, BATCH, SEQ, 64, a_conv_w, a_log_p, a_dtb_p, a_nw, conv_zero, s_zero, l)
        oa_s, sa_s = gdn_mixer(proj, N_PROMPT, DEC_BATCH, DEC_SEQ, DEC_SEQ, a_conv_w, a_log_p, a_dtb_p, a_nw,
                               conv_s[l], state_a[l], l)
        ob_p = attn_prompt(proj, COL_B_Q, COL_B_K, COL_B_V, B_HEADS, B_HEADS, _multiplicity_dilated, None, None)
        ob_s = attn_sample(proj, COL_B_Q, COL_B_K, COL_B_V, B_HEADS, B_HEADS, _multiplicity_dilated,
                           bk_c, bv_c, l, None, 256)
        yc_p, cre_p, cim_p = s5_scan(proj, 0, BATCH, 8, SEQ // 8, True, cw, h_zero, h_zero, l)
        yc_s, cre_s, cim_s = s5_scan(proj, N_PROMPT, 1, DEC_BATCH, DEC_SEQ, False, cw, hc_re[l], hc_im[l], l)
        od_p = attn_prompt(proj, COL_D_Q, COL_D_K, COL_D_V, D_HEADS, D_KV_HEADS, _multiplicity_window, 1, sinks[l])
        od_s = attn_sample(proj, COL_D_Q, COL_D_K, COL_D_V, D_HEADS, D_KV_HEADS, _multiplicity_window,
                           dk_c, dv_c, l, sinks[l], D_WINDOW)

        cat = lambda a, b: jnp.concatenate([a, b], axis=0)
        oc = glu(cat(yc_p, yc_s), glu_w_b, glu_b, l)
        x1, x1t = out_proj((cat(oa_p, oa_s), cat(ob_p, ob_s), oc, cat(od_p, od_s)), x, w_out_b, ln1_g3, ln1_b3, l)
        st, tau, cz = peer_router(x1, wq_b, keys, l)
        y = peer_experts(x1t, u_b, vt_b, st, tau, cz, l)
        x = residual_ln(x1, y, ln2_g3, ln2_b3, l)

        heads = lambda t, h: t.reshape(t.shape[:2] + (h, HEAD_DIM))
        p_new = (pp[:, SEQ - (CONV_W - 1):, COL_A_QKV:COL_A_QKV + A_CONV_CH], sa_p,
                 heads(pp[:, SEQ - min(B_CACHE, SEQ):, COL_B_K:COL_B_K + B_WIDTH], B_HEADS),
                 heads(pp[:, SEQ - min(B_CACHE, SEQ):, COL_B_V:COL_B_V + B_WIDTH], B_HEADS),
                 cre_p.reshape(BATCH, C_GROUPS, C_STATE), cim_p.reshape(BATCH, C_GROUPS, C_STATE),
                 heads(pp[:, SEQ - D_WINDOW:, COL_D_K:COL_D_K + D_KV_WIDTH], D_KV_HEADS),
                 heads(pp[:, SEQ - D_WINDOW:, COL_D_V:COL_D_V + D_KV_WIDTH], D_KV_HEADS))
        s_new = (ps[:, DEC_SEQ - (CONV_W - 1):, COL_A_QKV:COL_A_QKV + A_CONV_CH], sa_s,
                 heads(ps[:, :, COL_B_K:COL_B_K + B_WIDTH], B_HEADS),
                 heads(ps[:, :, COL_B_V:COL_B_V + B_WIDTH], B_HEADS),
                 cre_s.reshape(DEC_BATCH, C_GROUPS, C_STATE), cim_s.reshape(DEC_BATCH, C_GROUPS, C_STATE),
                 heads(ps[:, :, COL_D_K:COL_D_K + D_KV_WIDTH], D_KV_HEADS),
                 heads(ps[:, :, COL_D_V:COL_D_V + D_KV_WIDTH], D_KV_HEADS))
        for i in range(8):
            p_states[i].append(p_new[i])
            s_states[i].append(s_new[i])

    y_prompt = x[:N_PROMPT].reshape(BATCH, SEQ, D_MODEL)
    y_sample = x[N_PROMPT:].reshape(DEC_BATCH, DEC_SEQ, D_MODEL)
    return (y_prompt, y_sample, *[jnp.stack(t) for t in p_states], *[jnp.stack(t) for t in s_states])
```

```python
import functools
import math

import jax
import jax.numpy as jnp
from jax import lax
from jax.experimental import pallas as pl
from jax.experimental.pallas import tpu as pltpu

F32 = jnp.float32
BF16 = jnp.bfloat16
HIGHEST = lax.Precision.HIGHEST

D_MODEL = 2048
BATCH = 4
SEQ = 2048
DEPTH = 4
DEC_BATCH = 32
DEC_SEQ = 8
HEAD_DIM = 64
A_HEADS = 4
A_DK = 128
A_QK = A_HEADS * A_DK
A_VW = A_HEADS * A_DK
A_CONV_CH = 2 * A_QK + A_VW
CONV_W = 4
B_HEADS = 8
B_WIDTH = B_HEADS * HEAD_DIM
B_CACHE = 2048
C_WIDTH = 512
C_GROUP = 16
C_GROUPS = C_WIDTH // C_GROUP
C_STATE = 64
D_HEADS = 8
D_KV_HEADS = 2
D_WIDTH = D_HEADS * HEAD_DIM
D_KV_WIDTH = D_KV_HEADS * HEAD_DIM
D_WINDOW = 128
PEER_HEADS = 8
N_KEYS = 128
N_EXPERTS = N_KEYS * N_KEYS
PEER_TOPK = 16
DEEPNORM_ALPHA = (2 * DEPTH) ** 0.25
ATTN_SCALE = HEAD_DIM ** -0.5
LN_EPS = 1e-5
RMS_EPS = 1e-6

N_PROMPT = BATCH * SEQ
N_SAMPLE = DEC_BATCH * DEC_SEQ
N_TOK = N_PROMPT + N_SAMPLE

SUBLANES = 8
LANES = 128
VMEM_LIMIT = 56 * 1024 * 1024

COL_A_QKV = 0
COL_A_Z = 1536
COL_B_Q = 2048
COL_B_K = 2560
COL_B_V = 3072
COL_C_U = 3584
COL_D_Q = 4096
COL_D_K = 4608
COL_D_V = 4736
COL_A_AB = 4864
PROJ_COLS = 4992

NEG_BIG = -1e30


def _silu(x):
    return x * jax.nn.sigmoid(x)


def _softplus(x):
    return jnp.maximum(x, 0.0) + jnp.log1p(jnp.exp(-jnp.abs(x)))


def _dot(a, b, precision=None):
    return jnp.dot(a, b, preferred_element_type=F32, precision=precision)


def _dot_nt(a, b, precision=None):
    return lax.dot_general(a, b, (((1,), (1,)), ((), ())), preferred_element_type=F32, precision=precision)


def _split_bf16(a):
    hi = a.astype(BF16)
    return hi, (a - hi.astype(F32)).astype(BF16)


def _dot3(a, b, nt=False):
    dot = _dot_nt if nt else _dot
    a_hi, a_lo = _split_bf16(a)
    b_hi, b_lo = _split_bf16(b)
    return dot(a_hi, b_hi) + (dot(a_hi, b_lo) + dot(a_lo, b_hi))


def _layer_norm(x, g, b):
    mu = jnp.mean(x, axis=-1, keepdims=True)
    xc = x - mu
    var = jnp.mean(xc * xc, axis=-1, keepdims=True)
    return xc * lax.rsqrt(var + LN_EPS) * g + b


def _params(*sem):
    return pltpu.CompilerParams(dimension_semantics=sem, vmem_limit_bytes=VMEM_LIMIT)


def _in_proj_kernel(x_ref, w_ref, o_ref, xb_ref):
    @pl.when(pl.program_id(1) == 0)
    def _():
        xb_ref[...] = x_ref[...].astype(BF16)

    o_ref[...] = _dot(xb_ref[...], w_ref[...])


def in_proj(x, w_all, layer, tm=768, tn=1664):
    n, k = x.shape
    m = w_all.shape[2]
    return pl.pallas_call(
        _in_proj_kernel,
        grid=(n // tm, m // tn),
        in_specs=[pl.BlockSpec((tm, k), lambda i, j: (i, 0)),
                  pl.BlockSpec((None, k, tn), lambda i, j: (layer, 0, j))],
        out_specs=pl.BlockSpec((tm, tn), lambda i, j: (i, j)),
        out_shape=jax.ShapeDtypeStruct((n, m), F32),
        scratch_shapes=[pltpu.VMEM((tm, k), BF16)],
        compiler_params=_params("parallel", "arbitrary"),
        name="in_proj",
    )(x, w_all)


def _unit_lower_inverse(lmat, c):
    blk = min(16, c)
    row = lax.broadcasted_iota(jnp.int32, (c, c), 0)
    col = lax.broadcasted_iota(jnp.int32, (c, c), 1)
    eye = (row == col).astype(F32)
    if c > blk:
        same = (row // blk) == (col // blk)
        ld = jnp.where(same, lmat, 0.0)
        lo = jnp.where(same, 0.0, lmat)
    else:
        ld, lo = lmat, None
    td = eye - ld
    pw = ld
    p = 1
    while 2 * p < blk:
        pw = _dot3(pw, pw)
        p *= 2
        td = td + _dot3(td, pw)
    if lo is None:
        return td
    assert c // blk == 4
    mm = _dot3(td, lo)
    m2 = _dot3(mm, mm)
    y = td - _dot3(mm, td)
    return y + _dot3(m2, y)


def _gdn_kernel(qkv_ref, z_ref, ab_ref, cw_ref, alog_ref, dtb_ref, nw_ref, cbuf_ref, s0_ref,
                o_ref, sfin_ref, ext_ref, s_ref, *, c):
    n = pl.program_id(1)

    @pl.when(n == 0)
    def _():
        ext_ref[0:SUBLANES, :] = cbuf_ref[...]
        s_ref[...] = s0_ref[...]

    ext_ref[SUBLANES:SUBLANES + c, :] = qkv_ref[...]
    cw = cw_ref[...]
    first = SUBLANES - (CONV_W - 1)
    conv = ext_ref[pl.ds(first, c), :] * cw[0:1, :]
    for j in range(1, CONV_W):
        conv = conv + ext_ref[pl.ds(first + j, c), :] * cw[j:j + 1, :]
    carry = ext_ref[c:c + SUBLANES, :]
    ext_ref[0:SUBLANES, :] = carry
    conv = _silu(conv)

    ab = ab_ref[...]
    g = -jnp.exp(alog_ref[...]) * _softplus(ab + dtb_ref[...])
    beta = jax.nn.sigmoid(ab)
    row = lax.broadcasted_iota(jnp.int32, (c, c), 0)
    col = lax.broadcasted_iota(jnp.int32, (c, c), 1)
    causal = row >= col
    strict = row > col
    gc = _dot(causal.astype(F32), g, HIGHEST)
    gct = gc.T
    z = z_ref[...]
    nw = nw_ref[...]

    for h in range(A_HEADS):
        q = conv[:, h * A_DK:(h + 1) * A_DK]
        k = conv[:, A_QK + h * A_DK:A_QK + (h + 1) * A_DK]
        v = conv[:, 2 * A_QK + h * A_DK:2 * A_QK + (h + 1) * A_DK]
        q = q * lax.rsqrt(jnp.sum(q * q, axis=-1, keepdims=True) + RMS_EPS) * (A_DK ** -0.5)
        k = k * lax.rsqrt(jnp.sum(k * k, axis=-1, keepdims=True) + RMS_EPS)
        bcol = beta[:, A_HEADS + h:A_HEADS + h + 1]
        gcol = gc[:, h:h + 1]
        grow = gct[h:h + 1, :]
        decay = jnp.where(causal, jnp.exp(jnp.where(causal, gcol - grow, 0.0)), 0.0)
        kb = k * bcol
        lmat = jnp.where(strict, _dot_nt(kb, k) * decay, 0.0)
        tinv = _unit_lower_inverse(lmat, c)
        eg = jnp.exp(gcol)
        u = _dot3(tinv, v * bcol)
        w = _dot3(tinv, kb * eg)
        qk = _dot_nt(q, k) * decay
        glast = gc[c - 1:c, h:h + 1]
        kd = k * jnp.exp(glast - gcol)
        s = s_ref[h]
        v_new = u - _dot(w, s)
        o = _dot(q * eg, s) + _dot(qk, v_new)
        s_ref[h] = s * jnp.exp(glast) + _dot(kd.T, v_new)
        o = o * lax.rsqrt(jnp.mean(o * o, axis=-1, keepdims=True) + RMS_EPS) * nw
        o_ref[:, h * A_DK:(h + 1) * A_DK] = o * _silu(z[:, h * A_DK:(h + 1) * A_DK])

    @pl.when(n == pl.num_programs(1) - 1)
    def _():
        sfin_ref[...] = s_ref[...]


def gdn_mixer(proj, row0, nb, seq, c, conv_w, a_log, dt_bias, norm_w, conv_buf, s0, layer):
    nchunks = seq // c
    rb0 = row0 // c
    kern = functools.partial(_gdn_kernel, c=c)
    return pl.pallas_call(
        kern,
        grid=(nb, nchunks),
        in_specs=[
            pl.BlockSpec((c, A_CONV_CH), lambda b, n: (rb0 + b * nchunks + n, COL_A_QKV // A_CONV_CH)),
            pl.BlockSpec((c, A_VW), lambda b, n: (rb0 + b * nchunks + n, COL_A_Z // A_VW)),
            pl.BlockSpec((c, LANES), lambda b, n: (rb0 + b * nchunks + n, COL_A_AB // LANES)),
            pl.BlockSpec((None, CONV_W, A_CONV_CH), lambda b, n: (layer, 0, 0)),
            pl.BlockSpec((None, 1, LANES), lambda b, n: (layer, 0, 0)),
            pl.BlockSpec((None, 1, LANES), lambda b, n: (layer, 0, 0)),
            pl.BlockSpec((None, 1, A_DK), lambda b, n: (layer, 0, 0)),
            pl.BlockSpec((None, SUBLANES, A_CONV_CH), lambda b, n: (b, 0, 0)),
            pl.BlockSpec((None, A_HEADS, A_DK, A_DK), lambda b, n: (b, 0, 0, 0)),
        ],
        out_specs=[
            pl.BlockSpec((c, A_VW), lambda b, n: (b * nchunks + n, 0)),
            pl.BlockSpec((None, A_HEADS, A_DK, A_DK), lambda b, n: (b, 0, 0, 0)),
        ],
        out_shape=[jax.ShapeDtypeStruct((nb * seq, A_VW), F32),
                   jax.ShapeDtypeStruct((nb, A_HEADS, A_DK, A_DK), F32)],
        scratch_shapes=[pltpu.VMEM((c + SUBLANES, A_CONV_CH), F32),
                        pltpu.VMEM((A_HEADS, A_DK, A_DK), F32)],
        compiler_params=_params("parallel", "arbitrary"),
        name=f"gdn_c{c}",
    )(proj, proj, proj, conv_w, a_log, dt_bias, norm_w, conv_buf, s0)


def _multiplicity_dilated(dist):
    nonneg = dist >= 0
    c1 = nonneg & (dist <= 128)
    c2 = nonneg & (dist <= 512) & ((dist & 3) == 0)
    c3 = nonneg & (dist <= 2048) & ((dist & 15) == 0)
    return c1.astype(F32) + c2.astype(F32) + c3.astype(F32)


def _multiplicity_window(dist):
    return ((dist >= 0) & (dist < D_WINDOW)).astype(F32)


def _online_update(s, mult, v_blk, m_old, l_old, acc_old):
    s = jnp.where(mult > 0.0, s, NEG_BIG)
    m_new = jnp.maximum(m_old, jnp.max(s, axis=-1, keepdims=True))
    alpha = jnp.exp(m_old - m_new)
    p = jnp.exp(s - m_new) * mult
    l_new = alpha * l_old + jnp.sum(p, axis=-1, keepdims=True)
    acc_new = alpha * acc_old + _dot(p, v_blk)
    return m_new, l_new, acc_new


def _attn_prompt_kernel(q_ref, k_ref, v_ref, sink_ref, o_ref, m_ref, l_ref, acc_ref,
                        *, tq, tk, n_heads, group, mult_fn, back_blocks, use_sink):
    qi = pl.program_id(1)
    m_ref[...] = jnp.full(m_ref.shape, NEG_BIG, F32)
    l_ref[...] = jnp.zeros(l_ref.shape, F32)
    acc_ref[...] = jnp.zeros(acc_ref.shape, F32)
    lo = 0 if back_blocks is None else jnp.maximum(qi - back_blocks, 0)

    def body(ki, carry):
        k0 = pl.multiple_of(ki * tk, tk)
        dist = (qi * tq + lax.broadcasted_iota(jnp.int32, (tq, tk), 0)) - (
            ki * tk + lax.broadcasted_iota(jnp.int32, (tq, tk), 1))
        mult = mult_fn(dist)
        for h in range(n_heads):
            hk = h // group
            q = q_ref[:, h * HEAD_DIM:(h + 1) * HEAD_DIM]
            kb = k_ref[pl.ds(k0, tk), hk * HEAD_DIM:(hk + 1) * HEAD_DIM]
            vb = v_ref[pl.ds(k0, tk), hk * HEAD_DIM:(hk + 1) * HEAD_DIM]
            s = _dot_nt(q, kb) * ATTN_SCALE
            m_new, l_new, acc_new = _online_update(s, mult, vb, m_ref[h], l_ref[h], acc_ref[h])
            m_ref[h] = m_new
            l_ref[h] = l_new
            acc_ref[h] = acc_new
        return carry

    lax.fori_loop(lo, qi + 1, body, 0)
    for h in range(n_heads):
        if use_sink:
            sk = sink_ref[h]
            m = m_ref[h]
            mm = jnp.maximum(m, sk)
            scale = jnp.exp(m - mm)
            den = l_ref[h] * scale + jnp.exp(sk - mm)
            out = acc_ref[h] * (scale / den)
        else:
            out = acc_ref[h] / l_ref[h]
        o_ref[:, h * HEAD_DIM:(h + 1) * HEAD_DIM] = out


def attn_prompt(proj, col_q, col_k, col_v, n_heads, n_kv, mult_fn, back_blocks, sinks, tq=256, tk=256):
    qw = n_heads * HEAD_DIM
    kw = n_kv * HEAD_DIM
    nq = SEQ // tq
    use_sink = sinks is not None
    if sinks is None:
        sinks = jnp.zeros((n_heads,), F32)
    kern = functools.partial(_attn_prompt_kernel, tq=tq, tk=tk, n_heads=n_heads, group=n_heads // n_kv,
                             mult_fn=mult_fn, back_blocks=back_blocks, use_sink=use_sink)
    return pl.pallas_call(
        kern,
        grid=(BATCH, nq),
        in_specs=[
            pl.BlockSpec((tq, qw), lambda b, i: (b * nq + i, col_q // qw)),
            pl.BlockSpec((SEQ, kw), lambda b, i: (b, col_k // kw)),
            pl.BlockSpec((SEQ, kw), lambda b, i: (b, col_v // kw)),
            pl.BlockSpec(memory_space=pltpu.SMEM),
        ],
        out_specs=pl.BlockSpec((tq, qw), lambda b, i: (b * nq + i, 0)),
        out_shape=jax.ShapeDtypeStruct((N_PROMPT, qw), F32),
        scratch_shapes=[pltpu.VMEM((n_heads, tq, 1), F32), pltpu.VMEM((n_heads, tq, 1), F32),
                        pltpu.VMEM((n_heads, tq, HEAD_DIM), F32)],
        compiler_params=_params("parallel", "arbitrary"),
        name=f"attn_prompt_{n_kv}",
    )(proj, proj, proj, sinks)


def _attn_sample_kernel(q_ref, kn_ref, vn_ref, kp_ref, vp_ref, sink_ref, o_ref,
                        *, tk, n_heads, n_kv, mult_fn, use_sink):
    t = DEC_SEQ
    group = n_heads // n_kv
    rows = group * t
    past = kp_ref.shape[0]
    trow = lax.broadcasted_iota(jnp.int32, (rows, 1), 0) % t
    for hk in range(n_kv):
        q = jnp.concatenate(
            [q_ref[:, (hk * group + g) * HEAD_DIM:(hk * group + g + 1) * HEAD_DIM] for g in range(group)], axis=0)
        m = jnp.full((rows, 1), NEG_BIG, F32)
        l = jnp.zeros((rows, 1), F32)
        acc = jnp.zeros((rows, HEAD_DIM), F32)
        for kb in range(past // tk):
            kblk = kp_ref[kb * tk:(kb + 1) * tk, hk * HEAD_DIM:(hk + 1) * HEAD_DIM]
            vblk = vp_ref[kb * tk:(kb + 1) * tk, hk * HEAD_DIM:(hk + 1) * HEAD_DIM]
            dist = (past + trow) - (kb * tk + lax.broadcasted_iota(jnp.int32, (rows, tk), 1))
            s = _dot_nt(q, kblk) * ATTN_SCALE
            m, l, acc = _online_update(s, mult_fn(dist), vblk, m, l, acc)
        kblk = kn_ref[:, hk * HEAD_DIM:(hk + 1) * HEAD_DIM]
        vblk = vn_ref[:, hk * HEAD_DIM:(hk + 1) * HEAD_DIM]
        dist = trow - lax.broadcasted_iota(jnp.int32, (rows, t), 1)
        s = _dot_nt(q, kblk) * ATTN_SCALE
        m, l, acc = _online_update(s, mult_fn(dist), vblk, m, l, acc)
        for g in range(group):
            h = hk * group + g
            mg, lg, ag = m[g * t:(g + 1) * t], l[g * t:(g + 1) * t], acc[g * t:(g + 1) * t]
            if use_sink:
                sk = sink_ref[h]
                mm = jnp.maximum(mg, sk)
                scale = jnp.exp(mg - mm)
                den = lg * scale + jnp.exp(sk - mm)
                out = ag * (scale / den)
            else:
                out = ag / lg
            o_ref[:, h * HEAD_DIM:(h + 1) * HEAD_DIM] = out


def attn_sample(proj, col_q, col_k, col_v, n_heads, n_kv, mult_fn, k_cache, v_cache, layer, sinks, tk):
    qw = n_heads * HEAD_DIM
    kw = n_kv * HEAD_DIM
    past = k_cache.shape[2]
    rb0 = N_PROMPT // DEC_SEQ
    use_sink = sinks is not None
    if sinks is None:
        sinks = jnp.zeros((n_heads,), F32)
    kern = functools.partial(_attn_sample_kernel, tk=tk, n_heads=n_heads, n_kv=n_kv, mult_fn=mult_fn,
                             use_sink=use_sink)
    return pl.pallas_call(
        kern,
        grid=(DEC_BATCH,),
        in_specs=[
            pl.BlockSpec((DEC_SEQ, qw), lambda b: (rb0 + b, col_q // qw)),
            pl.BlockSpec((DEC_SEQ, kw), lambda b: (rb0 + b, col_k // kw)),
            pl.BlockSpec((DEC_SEQ, kw), lambda b: (rb0 + b, col_v // kw)),
            pl.BlockSpec((None, None, past, kw), lambda b: (layer, b, 0, 0)),
            pl.BlockSpec((None, None, past, kw), lambda b: (layer, b, 0, 0)),
            pl.BlockSpec(memory_space=pltpu.SMEM),
        ],
        out_specs=pl.BlockSpec((DEC_SEQ, qw), lambda b: (b, 0)),
        out_shape=jax.ShapeDtypeStruct((N_SAMPLE, qw), F32),
        compiler_params=_params("parallel"),
        name=f"attn_sample_{n_kv}",
    )(proj, proj, proj, k_cache, v_cache, sinks)


C_COLBLK = LANES
C_GRPBLK = C_COLBLK // C_GROUP
C_STBLK = C_GRPBLK * C_STATE
S5_UNROLL = 8


def _cmul(ar, ai, br, bi):
    return ar * br - ai * bi, ar * bi + ai * br


def _s5_kernel(u_ref, bre_ref, bim_ref, are_ref, aim_ref, cre_ref, cim_ref, d_ref, h0re_ref, h0im_ref,
               y_ref, sre_ref, sim_ref, up_ref, xre_ref, xim_ref, *, streams, steps, chained):
    nlt = C_STBLK // LANES
    lt = lambda k: slice(k * LANES, (k + 1) * LANES)

    def permute_in(t, carry):
        up_ref[pl.ds(pl.multiple_of(t * streams, streams), streams), :] = u_ref[pl.ds(t, streams, stride=steps), :]
        return carry

    lax.fori_loop(0, steps, permute_in, 0, unroll=S5_UNROLL)
    u = up_ref[...]
    xre_ref[...] = _dot3(u, bre_ref[...])
    xim_ref[...] = _dot3(u, bim_ref[...])
    a_re = are_ref[...]
    a_im = aim_ref[...]

    def time_rows(t):
        return pl.ds(pl.multiple_of(t * streams, streams), streams)

    def advance(t, hr, hi):
        nr, ni = _cmul(a_re, a_im, hr, hi)
        return nr + xre_ref[time_rows(t), :], ni + xim_ref[time_rows(t), :]

    if chained:
        zero = jnp.zeros((streams, C_STBLK), F32)
        er, ei = lax.fori_loop(0, steps, lambda t, c: advance(t, *c), (zero, zero), unroll=S5_UNROLL)
        pr, pi = a_re, a_im
        p = 1
        while p < steps:
            pr, pi = _cmul(pr, pi, pr, pi)
            p *= 2
        assert p == steps
        sub = lax.broadcasted_iota(jnp.int32, (streams, C_STBLK), 0)
        cr = jnp.zeros((1, C_STBLK), F32)
        ci = jnp.zeros((1, C_STBLK), F32)
        hr, hi = zero, zero
        for s in range(1, streams):
            nr, ni = _cmul(pr, pi, cr, ci)
            cr = nr + er[s - 1:s, :]
            ci = ni + ei[s - 1:s, :]
            hr = jnp.where(sub == s, cr, hr)
            hi = jnp.where(sub == s, ci, hi)
    else:
        hr = h0re_ref[...]
        hi = h0im_ref[...]

    def store_step(t, c):
        nr, ni = advance(t, *c)
        xre_ref[time_rows(t), :] = nr
        xim_ref[time_rows(t), :] = ni
        return nr, ni

    hr, hi = lax.fori_loop(0, steps, store_step, (hr, hi), unroll=S5_UNROLL)
    if chained:
        sre_ref[...] = hr[streams - 1:streams, :]
        sim_ref[...] = hi[streams - 1:streams, :]
    else:
        sre_ref[...] = hr
        sim_ref[...] = hi
    up_ref[...] = d_ref[...] * u + _dot(xre_ref[...], cre_ref[...]) - _dot(xim_ref[...], cim_ref[...])

    def permute_out(t, carry):
        y_ref[pl.ds(t, streams, stride=steps), :] = up_ref[time_rows(t), :]
        return carry

    lax.fori_loop(0, steps, permute_out, 0, unroll=S5_UNROLL)


def s5_scan(proj, row0, nseq, streams, steps, chained, cw, h0_re, h0_im, layer):
    rows = streams * steps
    ncb = C_WIDTH // C_COLBLK
    rb0 = row0 // rows
    cb0 = COL_C_U // C_COLBLK
    srows = 1 if chained else streams
    kern = functools.partial(_s5_kernel, streams=streams, steps=steps, chained=chained)
    state_spec = pl.BlockSpec((None, srows, C_STBLK), lambda b, j: (b, 0, j))
    wspec = lambda r, c: pl.BlockSpec((None, None, r, c), lambda b, j: (layer, j, 0, 0))
    return pl.pallas_call(
        kern,
        grid=(nseq, ncb),
        in_specs=[
            pl.BlockSpec((rows, C_COLBLK), lambda b, j: (rb0 + b, cb0 + j)),
            wspec(C_COLBLK, C_STBLK), wspec(C_COLBLK, C_STBLK),
            pl.BlockSpec((None, 1, C_STBLK), lambda b, j: (layer, 0, j)),
            pl.BlockSpec((None, 1, C_STBLK), lambda b, j: (layer, 0, j)),
            wspec(C_STBLK, C_COLBLK), wspec(C_STBLK, C_COLBLK),
            pl.BlockSpec((None, 1, C_COLBLK), lambda b, j: (layer, 0, j)),
            state_spec, state_spec,
        ],
        out_specs=[pl.BlockSpec((rows, C_COLBLK), lambda b, j: (b, j)), state_spec, state_spec],
        out_shape=[jax.ShapeDtypeStruct((nseq * rows, C_WIDTH), F32),
                   jax.ShapeDtypeStruct((nseq, srows, C_GROUPS * C_STATE), F32),
                   jax.ShapeDtypeStruct((nseq, srows, C_GROUPS * C_STATE), F32)],
        scratch_shapes=[pltpu.VMEM((rows, C_COLBLK), F32), pltpu.VMEM((rows, C_STBLK), F32),
                        pltpu.VMEM((rows, C_STBLK), F32)],
        compiler_params=_params("parallel", "parallel"),
        name=f"s5_{streams}x{steps}",
    )(proj, cw["b_re"], cw["b_im"], cw["a_re"], cw["a_im"], cw["c_re"], cw["c_im"], cw["d"], h0_re, h0_im)


def s5_weights(lam_re, lam_im, log_step, b_re, b_im, c_re, c_im, d_skip):
    lr, li = lam_re.astype(F32), lam_im.astype(F32)
    dt = jnp.exp(log_step.astype(F32))[..., None]
    mag = jnp.exp(lr * dt)
    ar, ai = mag * jnp.cos(li * dt), mag * jnp.sin(li * dt)
    den = lr * lr + li * li
    fr = ((ar - 1.0) * lr + ai * li) / den
    fi = (ai * lr - (ar - 1.0) * li) / den
    br, bi = b_re.astype(F32), b_im.astype(F32)
    bbar_re = fr[..., None] * br - fi[..., None] * bi
    bbar_im = fr[..., None] * bi + fi[..., None] * br
    ncb = C_WIDTH // C_COLBLK
    eye = jnp.eye(C_GRPBLK, dtype=F32)

    def pack_in(t):
        t = jnp.swapaxes(t, -1, -2).reshape(DEPTH, ncb, C_GRPBLK, C_GROUP, C_STATE)
        return jnp.einsum("lbgcp,gh->lbgchp", t, eye).reshape(DEPTH, ncb, C_COLBLK, C_STBLK)

    def pack_out(t):
        t = jnp.swapaxes(t, -1, -2).reshape(DEPTH, ncb, C_GRPBLK, C_STATE, C_GROUP)
        return jnp.einsum("lbgpc,gh->lbgphc", t, eye).reshape(DEPTH, ncb, C_STBLK, C_COLBLK)

    flat = lambda t: t.reshape(DEPTH, 1, C_GROUPS * C_STATE)
    return {
        "a_re": flat(ar), "a_im": flat(ai),
        "b_re": pack_in(bbar_re), "b_im": pack_in(bbar_im),
        "c_re": pack_out(c_re.astype(F32)), "c_im": pack_out(c_im.astype(F32)),
        "d": d_skip.astype(F32).reshape(DEPTH, 1, C_WIDTH),
    }


def _glu_kernel(y_ref, w_ref, b_ref, o_ref):
    t = _dot(y_ref[...].astype(BF16), w_ref[...]) + b_ref[...]
    o_ref[...] = t[:, :C_WIDTH] * jax.nn.sigmoid(t[:, C_WIDTH:])


def glu(y, w_all, b_all, layer, tm=768):
    n = y.shape[0]
    return pl.pallas_call(
        _glu_kernel,
        grid=(n // tm,),
        in_specs=[pl.BlockSpec((tm, C_WIDTH), lambda i: (i, 0)),
                  pl.BlockSpec((None, C_WIDTH, 2 * C_WIDTH), lambda i: (layer, 0, 0)),
                  pl.BlockSpec((None, 1, 2 * C_WIDTH), lambda i: (layer, 0, 0))],
        out_specs=pl.BlockSpec((tm, C_WIDTH), lambda i: (i, 0)),
        out_shape=jax.ShapeDtypeStruct((n, C_WIDTH), F32),
        compiler_params=_params("parallel"),
        name="glu",
    )(y, w_all, b_all)


MIX_PARTS = 4
MIX_PART_W = D_MODEL // MIX_PARTS


def _out_proj_kernel(oa_ref, ob_ref, oc_ref, od_ref, x_ref, w_ref, g_ref, b_ref, x1_ref, xt_ref):
    acc = DEEPNORM_ALPHA * x_ref[...]
    for i, o_ref in enumerate((oa_ref, ob_ref, oc_ref, od_ref)):
        acc = acc + _dot(o_ref[...].astype(BF16), w_ref[i * MIX_PART_W:(i + 1) * MIX_PART_W, :])
    y = _layer_norm(acc, g_ref[...], b_ref[...])
    x1_ref[...] = y
    xt_ref[...] = y.T.astype(BF16)


def out_proj(o_parts, x, w_all, g_all, b_all, layer, tm=256):
    n = x.shape[0]
    part = pl.BlockSpec((tm, MIX_PART_W), lambda i: (i, 0))
    vec = pl.BlockSpec((None, 1, D_MODEL), lambda i: (layer, 0, 0))
    return pl.pallas_call(
        _out_proj_kernel,
        grid=(n // tm,),
        in_specs=[part, part, part, part,
                  pl.BlockSpec((tm, D_MODEL), lambda i: (i, 0)),
                  pl.BlockSpec((None, D_MODEL, D_MODEL), lambda i: (layer, 0, 0)),
                  vec, vec],
        out_specs=[pl.BlockSpec((tm, D_MODEL), lambda i: (i, 0)),
                   pl.BlockSpec((D_MODEL, tm), lambda i: (0, i))],
        out_shape=[jax.ShapeDtypeStruct((n, D_MODEL), F32), jax.ShapeDtypeStruct((D_MODEL, n), BF16)],
        compiler_params=_params("parallel"),
        name="out_proj",
    )(*o_parts, x, w_all, g_all, b_all)


def _max2(a, b):
    if a is None:
        return b
    if b is None:
        return a
    return jnp.maximum(a, b)


def _compare_exchange(a, b):
    if a is None:
        return b, None
    if b is None:
        return a, None
    return jnp.maximum(a, b), jnp.minimum(a, b)


def _bitonic_merge_desc(x):
    x = list(x)
    n = len(x)
    stride = n // 2
    while stride >= 1:
        for i in range(n):
            if (i & stride) == 0:
                x[i], x[i + stride] = _compare_exchange(x[i], x[i + stride])
        stride //= 2
    return x


def _sort_desc(x):
    n = len(x)
    if n == 1:
        return list(x)
    a = _sort_desc(x[:n // 2])
    b = _sort_desc(x[n // 2:])
    return _bitonic_merge_desc(a + b[::-1])


def _merge_top(a, b):
    n = len(a)
    return _bitonic_merge_desc([_max2(a[r], b[n - 1 - r]) for r in range(n)])


def _router_kernel(x_ref, wq_ref, keys_ref, st_ref, tau_ref, cz_ref):
    tm = x_ref.shape[0]
    q = _dot(x_ref[...].astype(BF16), wq_ref[...])
    sub = lax.broadcasted_iota(jnp.int32, (SUBLANES, tm), 0)
    nslab = N_KEYS // SUBLANES
    assert nslab == PEER_TOPK
    tops = [[None] * PEER_TOPK, [None] * PEER_TOPK]
    for h in range(PEER_HEADS):
        for c in range(2):
            hc = 2 * h + c
            st = _dot_nt(keys_ref[hc], q[:, hc * N_KEYS:(hc + 1) * N_KEYS])
            st_ref[hc] = st
            a = _sort_desc([st[r * SUBLANES:(r + 1) * SUBLANES, :] for r in range(nslab)])
            for shift in (4, 2, 1):
                a = _merge_top(a, [pltpu.roll(t, shift, 0) for t in a])
            for r in range(PEER_TOPK):
                tops[c][r] = a[r] if h == 0 else jnp.where(sub == h, a[r], tops[c][r])
    v1, v2 = tops
    best = [v1[0] + v2[j] for j in range(PEER_TOPK)]
    for i in range(1, PEER_TOPK):
        n_i = PEER_TOPK // (i + 1)
        best = _merge_top(best, [v1[i] + v2[j] for j in range(n_i)] + [None] * (PEER_TOPK - n_i))
    m0 = best[0]
    z = jnp.ones_like(m0)
    for r in range(1, PEER_TOPK):
        z = z + jnp.exp(best[r] - m0)
    tau_ref[...] = best[PEER_TOPK - 1]
    cz_ref[...] = m0 + jnp.log(z)


def peer_router(x1, wq_all, keys_all, layer, tm=256):
    n = x1.shape[0]
    nhc = 2 * PEER_HEADS
    return pl.pallas_call(
        _router_kernel,
        grid=(n // tm,),
        in_specs=[pl.BlockSpec((tm, D_MODEL), lambda i: (i, 0)),
                  pl.BlockSpec((None, D_MODEL, nhc * N_KEYS), lambda i: (layer, 0, 0)),
                  pl.BlockSpec((None, nhc, N_KEYS, N_KEYS), lambda i: (layer, 0, 0, 0))],
        out_specs=[pl.BlockSpec((nhc, N_KEYS, tm), lambda i: (0, 0, i)),
                   pl.BlockSpec((PEER_HEADS, tm), lambda i: (0, i)),
                   pl.BlockSpec((PEER_HEADS, tm), lambda i: (0, i))],
        out_shape=[jax.ShapeDtypeStruct((nhc, N_KEYS, n), F32),
                   jax.ShapeDtypeStruct((PEER_HEADS, n), F32),
                   jax.ShapeDtypeStruct((PEER_HEADS, n), F32)],
        compiler_params=_params("parallel"),
        name="peer_router",
    )(x1, wq_all, keys_all)


PEER_TM = 768
PEER_TE = 512
PEER_ROWS = 32
PEER_LANES = 256


def _gelu(x):
    return x * (lax.erf(x * (2.0 ** -0.5)) + 1.0) * 0.5


def _peer_kernel(xt_ref, u_ref, vt_ref, st_ref, tau_ref, cz_ref, y_ref, *scratch):
    j = pl.program_id(1)
    tm = xt_ref.shape[1]
    nlc = tm // PEER_LANES
    acc_refs, ht_refs, ct_refs = scratch[:nlc], scratch[nlc:2 * nlc], scratch[2 * nlc:]
    lanes = lambda lc: slice(lc * PEER_LANES, (lc + 1) * PEER_LANES)

    @pl.when(j == 0)
    def _():
        for acc_ref in acc_refs:
            acc_ref[...] = jnp.zeros(acc_ref.shape, F32)

    def hidden(lc):
        ht_refs[lc][...] = _dot(u_ref[...], xt_ref[:, lanes(lc)])

    def gated(lc):
        ls = lanes(lc)
        tau = [tau_ref[h:h + 1, ls] for h in range(PEER_HEADS)]
        cz = [cz_ref[h:h + 1, ls] for h in range(PEER_HEADS)]
        for ar in range(PEER_TE // N_KEYS):
            a = j * (PEER_TE // N_KEYS) + ar
            s1 = [st_ref[2 * h, pl.ds(a, 1), ls] for h in range(PEER_HEADS)]
            for rc in range(N_KEYS // PEER_ROWS):
                rs = slice(rc * PEER_ROWS, (rc + 1) * PEER_ROWS)
                gate = jnp.zeros((PEER_ROWS, PEER_LANES), F32)
                for h in range(PEER_HEADS):
                    sm = st_ref[2 * h + 1, rs, ls] + s1[h]
                    gate = gate + jnp.where(sm >= tau[h], jnp.exp(sm - cz[h]), 0.0)
                er = slice(ar * N_KEYS + rc * PEER_ROWS, ar * N_KEYS + (rc + 1) * PEER_ROWS)
                ct_refs[lc][er, :] = (_gelu(ht_refs[lc][er, :]) * gate).astype(BF16)

    def project(lc):
        acc_refs[lc][...] += _dot(vt_ref[...], ct_refs[lc][...])

    hidden(0)
    for lc in range(nlc):
        if lc + 1 < nlc:
            hidden(lc + 1)
        gated(lc)
        project(lc)

    @pl.when(j == pl.num_programs(1) - 1)
    def _():
        for lc in range(nlc):
            y_ref[lanes(lc), :] = acc_refs[lc][...].T


def peer_experts(xt, u_all, vt_all, st, tau, cz, layer):
    n = xt.shape[1]
    nhc = 2 * PEER_HEADS
    tm, te = PEER_TM, PEER_TE
    return pl.pallas_call(
        _peer_kernel,
        grid=(n // tm, N_EXPERTS // te),
        in_specs=[pl.BlockSpec((D_MODEL, tm), lambda i, j: (0, i)),
                  pl.BlockSpec((None, te, D_MODEL), lambda i, j: (layer, j, 0)),
                  pl.BlockSpec((None, D_MODEL, te), lambda i, j: (layer, 0, j)),
                  pl.BlockSpec((nhc, N_KEYS, tm), lambda i, j: (0, 0, i)),
                  pl.BlockSpec((PEER_HEADS, tm), lambda i, j: (0, i)),
                  pl.BlockSpec((PEER_HEADS, tm), lambda i, j: (0, i))],
        out_specs=pl.BlockSpec((tm, D_MODEL), lambda i, j: (i, 0)),
        out_shape=jax.ShapeDtypeStruct((n, D_MODEL), F32),
        scratch_shapes=([pltpu.VMEM((D_MODEL, PEER_LANES), F32)] * (tm // PEER_LANES)
                        + [pltpu.VMEM((te, PEER_LANES), F32)] * (tm // PEER_LANES)
                        + [pltpu.VMEM((te, PEER_LANES), BF16)] * (tm // PEER_LANES)),
        compiler_params=_params("parallel", "arbitrary"),
        name="peer_experts",
    )(xt, u_all, vt_all, st, tau, cz)


def _residual_ln_kernel(x_ref, y_ref, g_ref, b_ref, o_ref):
    o_ref[...] = _layer_norm(DEEPNORM_ALPHA * x_ref[...] + y_ref[...], g_ref[...], b_ref[...])


def residual_ln(x, y, g_all, b_all, layer, tm=256):
    n = x.shape[0]
    row = pl.BlockSpec((tm, D_MODEL), lambda i: (i, 0))
    vec = pl.BlockSpec((None, 1, D_MODEL), lambda i: (layer, 0, 0))
    return pl.pallas_call(
        _residual_ln_kernel,
        grid=(n // tm,),
        in_specs=[row, row, vec, vec],
        out_specs=row,
        out_shape=jax.ShapeDtypeStruct((n, D_MODEL), F32),
        compiler_params=_params("parallel"),
        name="residual_ln",
    )(x, y, g_all, b_all)


def _permute_w_in(w_in):
    ab0 = A_CONV_CH + A_VW
    ab1 = ab0 + 2 * A_HEADS
    pad = PROJ_COLS - w_in.shape[2]
    return jnp.concatenate(
        [w_in[:, :, :ab0], w_in[:, :, ab1:], w_in[:, :, ab0:ab1], jnp.zeros(w_in.shape[:2] + (pad,), w_in.dtype)],
        axis=2).astype(BF16)


def kernel(x_prompt, x_sample, cache_a_conv, state_a, cache_b_k, cache_b_v, state_c_re, state_c_im, cache_d_k, cache_d_v, w_in, a_conv_w, a_log, a_dt_bias, a_norm_w, c_lambda_re, c_lambda_im, c_log_step, c_b_re, c_b_im, c_c_re, c_c_im, c_d, c_glu_w, c_glu_b, d_sinks, w_out, ln1_g, ln1_b, peer_wq, peer_sub_keys, peer_u, peer_v, ln2_g, ln2_b):
    x = jnp.concatenate([x_prompt.reshape(N_PROMPT, D_MODEL), x_sample.reshape(N_SAMPLE, D_MODEL)], axis=0)

    w_in_b = _permute_w_in(w_in)
    w_out_b = w_out.astype(BF16)
    wq_b = peer_wq.astype(BF16)
    u_b = peer_u.astype(BF16)
    vt_b = jnp.swapaxes(peer_v, 1, 2).astype(BF16)
    glu_w_b = c_glu_w.astype(BF16)
    glu_b = c_glu_b.reshape(DEPTH, 1, 2 * C_WIDTH)
    keys = peer_sub_keys.reshape(DEPTH, 2 * PEER_HEADS, N_KEYS, N_KEYS)
    cw = s5_weights(c_lambda_re, c_lambda_im, c_log_step, c_b_re, c_b_im, c_c_re, c_c_im, c_d)
    lane_pad = lambda t: jnp.pad(t.astype(F32), ((0, 0), (0, LANES - t.shape[1])))[:, None, :]
    a_log_p = lane_pad(a_log)
    a_dtb_p = lane_pad(a_dt_bias)
    a_nw = a_norm_w.astype(F32)[:, None, :]
    vec = lambda t: t.astype(F32)[:, None, :]
    ln1_g3, ln1_b3, ln2_g3, ln2_b3 = vec(ln1_g), vec(ln1_b), vec(ln2_g), vec(ln2_b)
    sinks = d_sinks.reshape(DEPTH, D_HEADS)

    conv_zero = jnp.zeros((BATCH, SUBLANES, A_CONV_CH), F32)
    s_zero = jnp.zeros((BATCH, A_HEADS, A_DK, A_DK), F32)
    conv_s = jnp.pad(cache_a_conv, ((0, 0), (0, 0), (SUBLANES - (CONV_W - 1), 0), (0, 0)))
    bk_c = cache_b_k.reshape(DEPTH, DEC_BATCH, -1, B_WIDTH)
    bv_c = cache_b_v.reshape(DEPTH, DEC_BATCH, -1, B_WIDTH)
    dk_c = cache_d_k.reshape(DEPTH, DEC_BATCH, -1, D_KV_WIDTH)
    dv_c = cache_d_v.reshape(DEPTH, DEC_BATCH, -1, D_KV_WIDTH)
    hc_re = state_c_re.reshape(DEPTH, 1, DEC_BATCH, C_GROUPS * C_STATE)
    hc_im = state_c_im.reshape(DEPTH, 1, DEC_BATCH, C_GROUPS * C_STATE)
    h_zero = jnp.zeros((BATCH, 1, C_GROUPS * C_STATE), F32)

    p_states = [[] for _ in range(8)]
    s_states = [[] for _ in range(8)]
    for l in range(DEPTH):
        proj = in_proj(x, w_in_b, l)
        pp = proj[:N_PROMPT].reshape(BATCH, SEQ, PROJ_COLS)
        ps = proj[N_PROMPT:].reshape(DEC_BATCH, DEC_SEQ, PROJ_COLS)

        oa_p, sa_p = gdn_mixer(proj, 0, BATCH, SEQ, 64, a_conv_w, a_log_p, a_dtb_p, a_nw, conv_zero, s_zero, l)
        oa_s, sa_s = gdn_mixer(proj, N_PROMPT, DEC_BATCH, DEC_SEQ, DEC_SEQ, a_conv_w, a_log_p, a_dtb_p, a_nw,
                               conv_s[l], state_a[l], l)
        ob_p = attn_prompt(proj, COL_B_Q, COL_B_K, COL_B_V, B_HEADS, B_HEADS, _multiplicity_dilated, None, None)
        ob_s = attn_sample(proj, COL_B_Q, COL_B_K, COL_B_V, B_HEADS, B_HEADS, _multiplicity_dilated,
                           bk_c, bv_c, l, None, 256)
        yc_p, cre_p, cim_p = s5_scan(proj, 0, BATCH, 8, SEQ // 8, True, cw, h_zero, h_zero, l)
        yc_s, cre_s, cim_s = s5_scan(proj, N_PROMPT, 1, DEC_BATCH, DEC_SEQ, False, cw, hc_re[l], hc_im[l], l)
        od_p = attn_prompt(proj, COL_D_Q, COL_D_K, COL_D_V, D_HEADS, D_KV_HEADS, _multiplicity_window, 1, sinks[l])
        od_s = attn_sample(proj, COL_D_Q, COL_D_K, COL_D_V, D_HEADS, D_KV_HEADS, _multiplicity_window,
                           dk_c, dv_c, l, sinks[l], D_WINDOW)

        cat = lambda a, b: jnp.concatenate([a, b], axis=0)
        oc = glu(cat(yc_p, yc_s), glu_w_b, glu_b, l)
        x1, x1t = out_proj((cat(oa_p, oa_s), cat(ob_p, ob_s), oc, cat(od_p, od_s)), x, w_out_b, ln1_g3, ln1_b3, l)
        st, tau, cz = peer_router(x1, wq_b, keys, l)
        y = peer_experts(x1t, u_b, vt_b, st, tau, cz, l)
        x = residual_ln(x1, y, ln2_g3, ln2_b3, l)

        heads = lambda t, h: t.reshape(t.shape[:2] + (h, HEAD_DIM))
        p_new = (pp[:, SEQ - (CONV_W - 1):, COL_A_QKV:COL_A_QKV + A_CONV_CH], sa_p,
                 heads(pp[:, SEQ - min(B_CACHE, SEQ):, COL_B_K:COL_B_K + B_WIDTH], B_HEADS),
                 heads(pp[:, SEQ - min(B_CACHE, SEQ):, COL_B_V:COL_B_V + B_WIDTH], B_HEADS),
                 cre_p.reshape(BATCH, C_GROUPS, C_STATE), cim_p.reshape(BATCH, C_GROUPS, C_STATE),
                 heads(pp[:, SEQ - D_WINDOW:, COL_D_K:COL_D_K + D_KV_WIDTH], D_KV_HEADS),
                 heads(pp[:, SEQ - D_WINDOW:, COL_D_V:COL_D_V + D_KV_WIDTH], D_KV_HEADS))
        s_new = (ps[:, DEC_SEQ - (CONV_W - 1):, COL_A_QKV:COL_A_QKV + A_CONV_CH], sa_s,
                 heads(ps[:, :, COL_B_K:COL_B_K + B_WIDTH], B_HEADS),
                 heads(ps[:, :, COL_B_V:COL_B_V + B_WIDTH], B_HEADS),
                 cre_s.reshape(DEC_BATCH, C_GROUPS, C_STATE), cim_s.reshape(DEC_BATCH, C_GROUPS, C_STATE),
                 heads(ps[:, :, COL_D_K:COL_D_K + D_KV_WIDTH], D_KV_HEADS),
                 heads(ps[:, :, COL_D_V:COL_D_V + D_KV_WIDTH], D_KV_HEADS))
        for i in range(8):
            p_states[i].append(p_new[i])
            s_states[i].append(s_new[i])

    y_prompt = x[:N_PROMPT].reshape(BATCH, SEQ, D_MODEL)
    y_sample = x[N_PROMPT:].reshape(DEC_BATCH, DEC_SEQ, D_MODEL)
    return (y_prompt, y_sample, *[jnp.stack(t) for t in p_states], *[jnp.stack(t) for t in s_states])
```
